```python
import jax
import jax.numpy as jnp
from jax import lax
import numpy as np

D_MODEL = 2048
BATCH = 4
SEQ = 4096
DEPTH = 2
DEC_BATCH = 32
DEC_SEQ = 16
PAST_LEN = 4096

CHUNK = 64
N_MIXERS = 2
N_LAYERS_A = (DEPTH + N_MIXERS - 1) // N_MIXERS
N_LAYERS_B = DEPTH // N_MIXERS
HEAD_DIM_A = 64
N_HEADS_A = D_MODEL // HEAD_DIM_A
N_KV_A = N_HEADS_A // 8
GROUP_A = N_HEADS_A // N_KV_A
WINDOW = 128
WINDOW_CHUNKS = WINDOW // CHUNK
QKV_A = (N_HEADS_A + 2 * N_KV_A) * HEAD_DIM_A
ROPE_THETA = 10000.0
HEAD_DIM_B = 128
N_HEADS_B = D_MODEL // HEAD_DIM_B
SB_BLOCK = 128
N_GROUPS = 4
EXPERTS_PER_GROUP = 8
N_EXPERTS = N_GROUPS * EXPERTS_PER_GROUP
TOP_K = 2
D_EXPERT = 512
RMS_EPS = 1e-6
NEG_INF = -1e30

kernel_name = "chunk_causal_swa_sink_stickbreak_hmoe_step"


def rmsnorm(x, g):
    xf = x.astype(jnp.float32)
    y = xf * lax.rsqrt(jnp.mean(xf * xf, axis=-1, keepdims=True) + RMS_EPS) * g.astype(jnp.float32)
    return y.astype(x.dtype)


def rope(x, pos):
    hd = x.shape[-1]
    half = hd // 2
    inv = ROPE_THETA ** (-jnp.arange(half, dtype=jnp.float32) / half)
    ang = pos.astype(jnp.float32)[:, None] * inv[None, :]
    bshape = (1, x.shape[1]) + (1,) * (x.ndim - 3) + (half,)
    cos = jnp.cos(ang).reshape(bshape)
    sin = jnp.sin(ang).reshape(bshape)
    xf = x.astype(jnp.float32)
    x1, x2 = xf[..., :half], xf[..., half:]
    return jnp.concatenate([x1 * cos - x2 * sin, x2 * cos + x1 * sin], axis=-1).astype(x.dtype)


def window_visible(q_pos, k_pos):
    d = q_pos // CHUNK - k_pos // CHUNK
    return (k_pos >= 0) & (d >= 0) & (d <= WINDOW_CHUNKS)


def qkv_a(h, w_qkv, b_qkv, pos):
    B, T, _ = h.shape
    y = h @ w_qkv + b_qkv
    nq = N_HEADS_A * HEAD_DIM_A
    nk = N_KV_A * HEAD_DIM_A
    q = y[..., :nq].reshape(B, T, N_KV_A, GROUP_A, HEAD_DIM_A)
    k = y[..., nq:nq + nk].reshape(B, T, N_KV_A, HEAD_DIM_A)
    v = y[..., nq + nk:].reshape(B, T, N_KV_A, HEAD_DIM_A)
    return rope(q, pos), rope(k, pos), v


def sink_attn(q, k, v, mask, sinks):
    s = jnp.einsum('bnqkgd,bnskd->bnkgqs', q.astype(jnp.float32), k.astype(jnp.float32)) * (HEAD_DIM_A ** -0.5)
    s = jnp.where(mask[None, :, None, None], s, NEG_INF)
    sk = jnp.broadcast_to(sinks.astype(jnp.float32).reshape(1, 1, N_KV_A, GROUP_A, 1, 1), s.shape[:-1] + (1,))
    p = jax.nn.softmax(jnp.concatenate([s, sk], axis=-1), axis=-1)[..., :-1]
    return jnp.einsum('bnkgqs,bnskd->bnqkgd', p, v.astype(jnp.float32)).astype(q.dtype)


def window_attn_prompt(h, w_qkv, b_qkv, w_o, sinks):
    B, S, _ = h.shape
    pos = jnp.arange(S, dtype=jnp.int32)
    q, k, v = qkv_a(h, w_qkv, b_qkv, pos)
    n_chunks = S // CHUNK
    qb = q.reshape(B, n_chunks, CHUNK, N_KV_A, GROUP_A, HEAD_DIM_A)

    def band(t):
        tp = jnp.pad(t, ((0, 0), (WINDOW, 0), (0, 0), (0, 0)))
        tc = tp.reshape(B, n_chunks + WINDOW_CHUNKS, CHUNK, N_KV_A, HEAD_DIM_A)
        return jnp.concatenate([tc[:, j:j + n_chunks] for j in range(WINDOW_CHUNKS + 1)], axis=2)

    k_pos = (jnp.arange(n_chunks, dtype=jnp.int32) * CHUNK)[:, None] - WINDOW + \
        jnp.arange((WINDOW_CHUNKS + 1) * CHUNK, dtype=jnp.int32)[None, :]
    q_pos = pos.reshape(n_chunks, CHUNK)
    mask = window_visible(q_pos[:, :, None], k_pos[:, None, :])
    o = sink_attn(qb, band(k), band(v), mask, sinks)
    out = o.reshape(B, S, N_HEADS_A * HEAD_DIM_A) @ w_o
    return out, k[:, S - WINDOW:], v[:, S - WINDOW:]


def window_attn_sample(h, cache_k, cache_v, w_qkv, b_qkv, w_o, sinks):
    B, T, _ = h.shape
    pos = PAST_LEN + jnp.arange(T, dtype=jnp.int32)
    q, k, v = qkv_a(h, w_qkv, b_qkv, pos)
    kk = jnp.concatenate([cache_k.astype(k.dtype), k], axis=1)
    vv = jnp.concatenate([cache_v.astype(v.dtype), v], axis=1)
    k_pos = jnp.concatenate([PAST_LEN - WINDOW + jnp.arange(WINDOW, dtype=jnp.int32), pos])
    mask = window_visible(pos[:, None], k_pos[None, :])[None]
    o = sink_attn(q[:, None], kk[:, None], vv[:, None], mask, sinks)[:, 0]
    out = o.reshape(B, T, N_HEADS_A * HEAD_DIM_A) @ w_o
    return out, kk[:, -WINDOW:], vv[:, -WINDOW:]


def qkv_b(h, w_qkv):
    B, T, _ = h.shape
    y = (h @ w_qkv).reshape(B, T, 3, N_HEADS_B, HEAD_DIM_B)
    return y[:, :, 0], y[:, :, 1], y[:, :, 2]


def stick_breaking(q, k, v, q_pos, k_pos):
    z = jnp.einsum('bqhd,bkhd->bhqk', q.astype(jnp.float32), k.astype(jnp.float32)) * (HEAD_DIM_B ** -0.5)
    mask = (k_pos[None, :] < q_pos[:, None])[None, None]
    log_stay = jnp.where(mask, jax.nn.log_sigmoid(-z), 0.0)
    after = lax.cumsum(log_stay, axis=3, reverse=True) - log_stay
    a = jnp.where(mask, jnp.exp(jax.nn.log_sigmoid(z) + after), 0.0)
    return jnp.einsum('bhqk,bkhd->bqhd', a, v.astype(jnp.float32)).astype(q.dtype)


def stick_breaking_prompt(h, w_qkv, w_o):
    B, S, _ = h.shape
    pos = jnp.arange(S, dtype=jnp.int32)
    q, k, v = qkv_b(h, w_qkv)
    nb = S // SB_BLOCK
    qb = q.reshape(B, nb, SB_BLOCK, N_HEADS_B, HEAD_DIM_B).transpose(1, 0, 2, 3, 4)
    pb = pos.reshape(nb, SB_BLOCK)
    o = lax.map(lambda blk: stick_breaking(blk[0], k, v, blk[1], pos), (qb, pb))
    o = o.transpose(1, 0, 2, 3, 4).reshape(B, S, N_HEADS_B * HEAD_DIM_B)
    return o @ w_o, k, v


def stick_breaking_sample(h, cache_k, cache_v, w_qkv, w_o):
    B, T, _ = h.shape
    pos = PAST_LEN + jnp.arange(T, dtype=jnp.int32)
    q, k, v = qkv_b(h, w_qkv)
    kk = jnp.concatenate([cache_k.astype(k.dtype), k], axis=1)
    vv = jnp.concatenate([cache_v.astype(v.dtype), v], axis=1)
    k_pos = jnp.concatenate([jnp.arange(PAST_LEN, dtype=jnp.int32), pos])
    o = stick_breaking(q, kk, vv, pos, k_pos).reshape(B, T, N_HEADS_B * HEAD_DIM_B)
    return o @ w_o, k, v


def hier_moe(x, w_group, b_group, w_router, b_router, w_gate, w_up, w_down):
    n = x.shape[0]
    xf = x.astype(jnp.float32)
    g_logits = xf @ w_group.astype(jnp.float32) + b_group.astype(jnp.float32)
    g_prob = jax.nn.softmax(g_logits, axis=-1)
    g_idx = jnp.argmax(g_logits, axis=-1)
    g_w = jnp.take_along_axis(g_prob, g_idx[:, None], axis=1)
    e_logits = (xf @ w_router.astype(jnp.float32) + b_router.astype(jnp.float32)).reshape(n, N_GROUPS, EXPERTS_PER_GROUP)
    in_group = jnp.take_along_axis(e_logits, g_idx[:, None, None], axis=1)[:, 0]
    top_v, top_i = lax.top_k(in_group, TOP_K)
    top_w = jax.nn.softmax(top_v, axis=-1) * g_w
    e_ids = g_idx[:, None] * EXPERTS_PER_GROUP + top_i
    gates = jnp.sum(jax.nn.one_hot(e_ids, N_EXPERTS, dtype=jnp.float32) * top_w[..., None], axis=1)

    def expert_step(acc, p):
        wg, wu, wd, g = p
        hid = jax.nn.silu(x @ wg) * (x @ wu)
        return acc + (hid @ wd).astype(jnp.float32) * g[:, None], None

    acc, _ = lax.scan(expert_step, jnp.zeros((n, x.shape[1]), jnp.float32), (w_gate, w_up, w_down, gates.T))
    return acc.astype(x.dtype)


def setup_inputs(seed: int = 0) -> dict:
    key = jax.random.key(seed)
    ks = jax.random.split(key, 24)
    f32 = jnp.float32

    def nrm(k, shape, scale):
        return jax.random.normal(k, shape, f32) * scale

    return {
        "x_prompt": nrm(ks[0], (BATCH, SEQ, D_MODEL), 1.0),
        "x_sample": nrm(ks[1], (DEC_BATCH, DEC_SEQ, D_MODEL), 1.0),
        "cache_win_k": nrm(ks[2], (N_LAYERS_A, DEC_BATCH, WINDOW, N_KV_A, HEAD_DIM_A), 1.0),
        "cache_win_v": nrm(ks[3], (N_LAYERS_A, DEC_BATCH, WINDOW, N_KV_A, HEAD_DIM_A), 1.0),
        "cache_sb_k": nrm(ks[4], (N_LAYERS_B, DEC_BATCH, PAST_LEN, N_HEADS_B, HEAD_DIM_B), 1.0),
        "cache_sb_v": nrm(ks[5], (N_LAYERS_B, DEC_BATCH, PAST_LEN, N_HEADS_B, HEAD_DIM_B), 1.0),
        "norm_mix": 1.0 + nrm(ks[6], (DEPTH, D_MODEL), 0.02),
        "norm_ffn": 1.0 + nrm(ks[7], (DEPTH, D_MODEL), 0.02),
        "norm_final": 1.0 + nrm(ks[8], (D_MODEL,), 0.02),
        "a_w_qkv": nrm(ks[9], (N_LAYERS_A, D_MODEL, QKV_A), D_MODEL ** -0.5),
        "a_b_qkv": nrm(ks[10], (N_LAYERS_A, QKV_A), 0.02),
        "a_sinks": nrm(ks[11], (N_LAYERS_A, N_HEADS_A), 0.5),
        "a_w_o": nrm(ks[12], (N_LAYERS_A, N_HEADS_A * HEAD_DIM_A, D_MODEL), (N_HEADS_A * HEAD_DIM_A) ** -0.5),
        "b_w_qkv": nrm(ks[13], (N_LAYERS_B, D_MODEL, 3 * N_HEADS_B * HEAD_DIM_B), D_MODEL ** -0.5),
        "b_w_o": nrm(ks[14], (N_LAYERS_B, N_HEADS_B * HEAD_DIM_B, D_MODEL), (N_HEADS_B * HEAD_DIM_B) ** -0.5),
        "moe_w_group": nrm(ks[15], (DEPTH, D_MODEL, N_GROUPS), D_MODEL ** -0.5),
        "moe_b_group": nrm(ks[16], (DEPTH, N_GROUPS), 0.01),
        "moe_w_router": nrm(ks[17], (DEPTH, D_MODEL, N_EXPERTS), D_MODEL ** -0.5),
        "moe_b_router": nrm(ks[18], (DEPTH, N_EXPERTS), 0.01),
        "moe_w_gate": nrm(ks[19], (DEPTH, N_EXPERTS, D_MODEL, D_EXPERT), D_MODEL ** -0.5),
        "moe_w_up": nrm(ks[20], (DEPTH, N_EXPERTS, D_MODEL, D_EXPERT), D_MODEL ** -0.5),
        "moe_w_down": nrm(ks[21], (DEPTH, N_EXPERTS, D_EXPERT, D_MODEL), D_EXPERT ** -0.5),
    }


def reference(x_prompt, x_sample, cache_win_k, cache_win_v, cache_sb_k, cache_sb_v,
              norm_mix, norm_ffn, norm_final, a_w_qkv, a_b_qkv, a_sinks, a_w_o,
              b_w_qkv, b_w_o, moe_w_group, moe_b_group, moe_w_router, moe_b_router,
              moe_w_gate, moe_w_up, moe_w_down):
    xp, xs = x_prompt, x_sample
    Bp, Sp, D = xp.shape
    Bs, Ts, _ = xs.shape
    wkp, wvp, wks, wvs = [], [], [], []
    skp, svp, sks, svs = [], [], [], []
    for i in range(DEPTH):
        j = i // N_MIXERS
        hp = rmsnorm(xp, norm_mix[i])
        hs = rmsnorm(xs, norm_mix[i])
        if i % N_MIXERS == 0:
            op, kp, vp = window_attn_prompt(hp, a_w_qkv[j], a_b_qkv[j], a_w_o[j], a_sinks[j])
            osm, ksm, vsm = window_attn_sample(hs, cache_win_k[j], cache_win_v[j], a_w_qkv[j], a_b_qkv[j], a_w_o[j], a_sinks[j])
            wkp.append(kp); wvp.append(vp); wks.append(ksm); wvs.append(vsm)
        else:
            op, kp, vp = stick_breaking_prompt(hp, b_w_qkv[j], b_w_o[j])
            osm, ksm, vsm = stick_breaking_sample(hs, cache_sb_k[j], cache_sb_v[j], b_w_qkv[j], b_w_o[j])
            skp.append(kp); svp.append(vp); sks.append(ksm); svs.append(vsm)
        xp = xp + op
        xs = xs + osm
        tok = jnp.concatenate([rmsnorm(xp, norm_ffn[i]).reshape(Bp * Sp, D),
                               rmsnorm(xs, norm_ffn[i]).reshape(Bs * Ts, D)], axis=0)
        f = hier_moe(tok, moe_w_group[i], moe_b_group[i], moe_w_router[i], moe_b_router[i],
                     moe_w_gate[i], moe_w_up[i], moe_w_down[i])
        xp = xp + f[:Bp * Sp].reshape(Bp, Sp, D)
        xs = xs + f[Bp * Sp:].reshape(Bs, Ts, D)
    y_prompt = rmsnorm(xp, norm_final)
    y_sample = rmsnorm(xs, norm_final)
    return (y_prompt, y_sample,
            jnp.stack(wkp), jnp.stack(wvp), jnp.stack(wks), jnp.stack(wvs),
            jnp.stack(skp), jnp.stack(svp), jnp.stack(sks), jnp.stack(svs))
```

```python
import functools

import jax
import jax.numpy as jnp
from jax import lax
from jax.experimental import pallas as pl
from jax.experimental.pallas import tpu as pltpu

F32 = jnp.float32
BF16 = jnp.bfloat16

CHUNK = 64
WINDOW = 128
HEAD_DIM_A = 64
GROUP_A = 8
HEAD_DIM_B = 128
N_GROUPS = 4
EXPERTS_PER_GROUP = 8
N_EXPERTS = N_GROUPS * EXPERTS_PER_GROUP
ROPE_THETA = 10000.0
RMS_EPS = 1e-6
NEG_INF = -1e30

LANES = 128
ROUTER_LO_LANE = 64
GATE_LANE0 = N_GROUPS
VMEM_LIMIT = 56 * 1024 * 1024

NT_DIMS = (((1,), (1,)), ((), ()))


def _cparams(sem):
    return pltpu.CompilerParams(dimension_semantics=sem, vmem_limit_bytes=VMEM_LIMIT)


def _pick(n, pref):
    t = min(n, pref)
    while n % t:
        t //= 2
    return t


def _norm_proj_kernel(x_ref, g_ref, w_ref, b_ref, cos_ref, sin_ref, o_ref, h_ref, *, rope_cols, tn):
    j = pl.program_id(1)

    @pl.when(j == 0)
    def _():
        x = x_ref[...]
        ms = jnp.mean(x * x, axis=-1, keepdims=True)
        h_ref[...] = (x * lax.rsqrt(ms + RMS_EPS) * g_ref[...]).astype(BF16)

    y = jnp.dot(h_ref[...], w_ref[...], preferred_element_type=F32) + b_ref[...]
    if rope_cols:
        reps = tn // LANES
        cos = jnp.concatenate([cos_ref[...]] * reps, axis=1)
        sin = jnp.concatenate([sin_ref[...]] * reps, axis=1)
        lane = lax.broadcasted_iota(jnp.int32, y.shape, 1)
        half = HEAD_DIM_A // 2
        first = (lane % HEAD_DIM_A) < half
        swapped = jnp.where(first, pltpu.roll(y, tn - half, 1), pltpu.roll(y, half, 1))
        roped = y * cos + swapped * sin
        y = jnp.where(lane + j * tn < rope_cols, roped, y)
    o_ref[...] = y


def _norm_proj(x, g, w, b, cos, sin, rope_cols, tn):
    n, d = x.shape
    nout = w.shape[1]
    tm = _pick(n, 512)
    kern = functools.partial(_norm_proj_kernel, rope_cols=rope_cols, tn=tn)
    return pl.pallas_call(
        kern,
        grid=(n // tm, nout // tn),
        in_specs=[
            pl.BlockSpec((tm, d), lambda i, j: (i, 0)),
            pl.BlockSpec((1, d), lambda i, j: (0, 0)),
            pl.BlockSpec((d, tn), lambda i, j: (0, j)),
            pl.BlockSpec((1, tn), lambda i, j: (0, j)),
            pl.BlockSpec((tm, LANES), lambda i, j: (i, 0)),
            pl.BlockSpec((tm, LANES), lambda i, j: (i, 0)),
        ],
        out_specs=pl.BlockSpec((tm, tn), lambda i, j: (i, j)),
        out_shape=jax.ShapeDtypeStruct((n, nout), F32),
        scratch_shapes=[pltpu.VMEM((tm, d), BF16)],
        compiler_params=_cparams(("arbitrary", "arbitrary")),
        name="norm_proj",
    )(x, g.reshape(1, d), w, b.reshape(1, nout), cos, sin)


def _sink_attn_all_groups(q, k, v, sink_ref, valid, n_kv):
    t = q.shape[0]
    hd = HEAD_DIM_A
    scale = hd ** -0.5
    outs = []
    for g in range(n_kv):
        kg = k[:, g * hd:(g + 1) * hd].astype(BF16)
        vg = v[:, g * hd:(g + 1) * hd].astype(BF16)
        qs = jnp.concatenate(
            [q[:, (g * GROUP_A + h) * hd:(g * GROUP_A + h + 1) * hd] for h in range(GROUP_A)], axis=0)
        s = lax.dot_general(qs.astype(BF16), kg, NT_DIMS, preferred_element_type=F32) * scale
        if valid is not None:
            s = jnp.where(valid, s, NEG_INF)
        sink = sink_ref[g]
        m = jnp.maximum(jnp.max(s, axis=-1, keepdims=True), sink)
        p = jnp.exp(s - m)
        denom = jnp.sum(p, axis=-1, keepdims=True) + jnp.exp(sink - m)
        o = jnp.dot(p.astype(BF16), vg, preferred_element_type=F32) / denom
        outs.extend(o[h * t:(h + 1) * t] for h in range(GROUP_A))
    return jnp.concatenate(outs, axis=1)


def _win_prompt_kernel(q_ref, k0_ref, k1_ref, k2_ref, v0_ref, v1_ref, v2_ref, sink_ref, o_ref, *, n_kv):
    c = pl.program_id(1)
    k = jnp.concatenate([k0_ref[...], k1_ref[...], k2_ref[...]], axis=0)
    v = jnp.concatenate([v0_ref[...], v1_ref[...], v2_ref[...]], axis=0)
    col = lax.broadcasted_iota(jnp.int32, (1, 3 * CHUNK), 1)
    valid = col >= (2 - c) * CHUNK
    o_ref[...] = _sink_attn_all_groups(q_ref[...], k, v, sink_ref, valid, n_kv)


def _win_prompt(qkv, sink_rows, batch, seq, n_kv):
    nq = n_kv * GROUP_A * HEAD_DIM_A
    nk = n_kv * HEAD_DIM_A
    nc = seq // CHUNK
    kcol = nq // nk
    q_spec = pl.BlockSpec((CHUNK, nq), lambda b, c: (b * nc + c, 0))

    def kv_spec(back, col):
        return pl.BlockSpec((CHUNK, nk), lambda b, c: (b * nc + jnp.maximum(c - back, 0), col))

    return pl.pallas_call(
        functools.partial(_win_prompt_kernel, n_kv=n_kv),
        grid=(batch, nc),
        in_specs=[q_spec, kv_spec(2, kcol), kv_spec(1, kcol), kv_spec(0, kcol),
                  kv_spec(2, kcol + 1), kv_spec(1, kcol + 1), kv_spec(0, kcol + 1),
                  pl.BlockSpec(sink_rows.shape, lambda b, c: (0, 0, 0))],
        out_specs=pl.BlockSpec((CHUNK, nq), lambda b, c: (b * nc + c, 0)),
        out_shape=jax.ShapeDtypeStruct((batch * seq, nq), F32),
        compiler_params=_cparams(("arbitrary", "arbitrary")),
        name="win_prompt",
    )(qkv, qkv, qkv, qkv, qkv, qkv, qkv, sink_rows)


def _win_sample_kernel(q_ref, kn_ref, vn_ref, ck_ref, cv_ref, sink_ref, o_ref, *, n_kv):
    k = jnp.concatenate([ck_ref[0], kn_ref[...]], axis=0)
    v = jnp.concatenate([cv_ref[0], vn_ref[...]], axis=0)
    o_ref[...] = _sink_attn_all_groups(q_ref[...], k, v, sink_ref, None, n_kv)


def _win_sample(qkv, cache_k, cache_v, sink_rows, row0, batch, t, n_kv):
    nq = n_kv * GROUP_A * HEAD_DIM_A
    nk = n_kv * HEAD_DIM_A
    kcol = nq // nk
    blk0 = row0 // t
    return pl.pallas_call(
        functools.partial(_win_sample_kernel, n_kv=n_kv),
        grid=(batch,),
        in_specs=[pl.BlockSpec((t, nq), lambda b: (blk0 + b, 0)),
                  pl.BlockSpec((t, nk), lambda b: (blk0 + b, kcol)),
                  pl.BlockSpec((t, nk), lambda b: (blk0 + b, kcol + 1)),
                  pl.BlockSpec((1, WINDOW, nk), lambda b: (b, 0, 0)),
                  pl.BlockSpec((1, WINDOW, nk), lambda b: (b, 0, 0)),
                  pl.BlockSpec(sink_rows.shape, lambda b: (0, 0, 0))],
        out_specs=pl.BlockSpec((t, nq), lambda b: (b, 0)),
        out_shape=jax.ShapeDtypeStruct((batch * t, nq), F32),
        compiler_params=_cparams(("arbitrary",)),
        name="win_sample",
    )(qkv, qkv, qkv, cache_k, cache_v, sink_rows)


def _sb_block(z, mask, carry, v, u):
    sp = jnp.maximum(z, 0.0) + jnp.log1p(jnp.exp(-jnp.abs(z)))
    ls = -sp if mask is None else jnp.where(mask, -sp, 0.0)
    hi = ls.astype(BF16)
    lo = (ls - hi.astype(F32)).astype(BF16)
    after = (jnp.dot(hi, u, preferred_element_type=F32) + jnp.dot(lo, u, preferred_element_type=F32)) + carry
    a = jnp.exp(z - sp + after)
    if mask is not None:
        a = jnp.where(mask, a, 0.0)
    out = jnp.dot(a.astype(BF16), v, preferred_element_type=F32)
    return out, carry + jnp.sum(ls, axis=-1, keepdims=True)


def _sb_prompt_kernel(qi_ref, kj_ref, q_ref, k_ref, v_ref, u_ref, o_ref, acc_ref, carry_ref, *, tq, sub):
    s = pl.program_id(2)
    qi = qi_ref[s]
    kj = kj_ref[s]

    @pl.when(kj == qi)
    def _():
        acc_ref[...] = jnp.zeros_like(acc_ref)
        carry_ref[...] = jnp.zeros_like(carry_ref)

    q = q_ref[...].astype(BF16)
    scale = HEAD_DIM_B ** -0.5
    row = lax.broadcasted_iota(jnp.int32, (tq, sub), 0) + qi * tq
    u = u_ref[...]
    for part in reversed(range(tq // sub)):
        kb = k_ref[part * sub:(part + 1) * sub, :].astype(BF16)
        vb = v_ref[part * sub:(part + 1) * sub, :].astype(BF16)
        z = lax.dot_general(q, kb, NT_DIMS, preferred_element_type=F32) * scale
        col = lax.broadcasted_iota(jnp.int32, (tq, sub), 1) + (kj * tq + part * sub)
        out, carry = _sb_block(z, col < row, carry_ref[...], vb, u)
        acc_ref[...] += out
        carry_ref[...] = carry

    @pl.when(kj == 0)
    def _():
        o_ref[...] = acc_ref[...]


def _sb_prompt(qkv, u, batch, seq, n_heads):
    tq = _pick(seq, 512)
    sub = u.shape[0]
    nqb = seq // tq
    pairs = [(qi, kj) for qi in range(nqb) for kj in range(qi, -1, -1)]
    qi_arr = jnp.asarray([p[0] for p in pairs], jnp.int32)
    kj_arr = jnp.asarray([p[1] for p in pairs], jnp.int32)
    hd = HEAD_DIM_B
    grid_spec = pltpu.PrefetchScalarGridSpec(
        num_scalar_prefetch=2,
        grid=(batch, n_heads, len(pairs)),
        in_specs=[
            pl.BlockSpec((tq, hd), lambda b, h, s, qi, kj: (b * nqb + qi[s], h)),
            pl.BlockSpec((tq, hd), lambda b, h, s, qi, kj: (b * nqb + kj[s], n_heads + h)),
            pl.BlockSpec((tq, hd), lambda b, h, s, qi, kj: (b * nqb + kj[s], 2 * n_heads + h)),
            pl.BlockSpec((sub, sub), lambda b, h, s, qi, kj: (0, 0)),
        ],
        out_specs=pl.BlockSpec((tq, hd), lambda b, h, s, qi, kj: (b * nqb + qi[s], h)),
        scratch_shapes=[pltpu.VMEM((tq, hd), F32), pltpu.VMEM((tq, 1), F32)],
    )
    return pl.pallas_call(
        functools.partial(_sb_prompt_kernel, tq=tq, sub=sub),
        grid_spec=grid_spec,
        out_shape=jax.ShapeDtypeStruct((batch * seq, n_heads * hd), F32),
        compiler_params=_cparams(("arbitrary", "arbitrary", "arbitrary")),
        name="sb_prompt",
    )(qi_arr, kj_arr, qkv, qkv, qkv, u)


def _sb_sample_kernel(q_ref, kn_ref, vn_ref, ck_ref, cv_ref, u_ref, o_ref, qbd_ref, acc_ref, carry_ref,
                      *, t, n_heads, tk, sub, n_cache_blocks):
    kb = pl.program_id(1)
    hd = HEAD_DIM_B
    rows = n_heads * t
    d = n_heads * hd
    scale = hd ** -0.5
    u = u_ref[...]

    @pl.when(kb == 0)
    def _():
        q = q_ref[...]
        qt = jnp.concatenate([q] * n_heads, axis=0)
        r = lax.broadcasted_iota(jnp.int32, (rows, d), 0)
        c = lax.broadcasted_iota(jnp.int32, (rows, d), 1)
        qbd_ref[...] = jnp.where(r // t == c // hd, qt, 0.0).astype(BF16)
        pad = jnp.zeros((sub - t, d), F32)
        kn = jnp.concatenate([kn_ref[...], pad], axis=0).astype(BF16)
        vn = jnp.concatenate([vn_ref[...], pad], axis=0).astype(BF16)
        z = lax.dot_general(qbd_ref[...], kn, NT_DIMS, preferred_element_type=F32) * scale
        rr = lax.broadcasted_iota(jnp.int32, (rows, sub), 0) % t
        cc = lax.broadcasted_iota(jnp.int32, (rows, sub), 1)
        out, carry = _sb_block(z, cc < rr, jnp.zeros((rows, 1), F32), vn, u)
        acc_ref[...] = out
        carry_ref[...] = carry

    @pl.when(kb > 0)
    def _():
        for part in reversed(range(tk // sub)):
            kblk = ck_ref[0, part * sub:(part + 1) * sub, :].astype(BF16)
            vblk = cv_ref[0, part * sub:(part + 1) * sub, :].astype(BF16)
            z = lax.dot_general(qbd_ref[...], kblk, NT_DIMS, preferred_element_type=F32) * scale
            out, carry = _sb_block(z, None, carry_ref[...], vblk, u)
            acc_ref[...] += out
            carry_ref[...] = carry

    @pl.when(kb == n_cache_blocks)
    def _():
        acc = acc_ref[...]
        o_ref[...] = jnp.concatenate(
            [acc[h * t:(h + 1) * t, h * hd:(h + 1) * hd] for h in range(n_heads)], axis=1)


def _sb_sample(qkv, cache_k, cache_v, u, row0, batch, t, n_heads):
    hd = HEAD_DIM_B
    d = n_heads * hd
    past = cache_k.shape[1]
    sub = u.shape[0]
    tk = _pick(past, 512)
    ncb = past // tk
    blk0 = row0 // t
    rows = n_heads * t

    def cache_idx(b, kb):
        return (b, ncb - jnp.maximum(kb, 1), 0)

    return pl.pallas_call(
        functools.partial(_sb_sample_kernel, t=t, n_heads=n_heads, tk=tk, sub=sub, n_cache_blocks=ncb),
        grid=(batch, ncb + 1),
        in_specs=[pl.BlockSpec((t, d), lambda b, kb: (blk0 + b, 0)),
                  pl.BlockSpec((t, d), lambda b, kb: (blk0 + b, 1)),
                  pl.BlockSpec((t, d), lambda b, kb: (blk0 + b, 2)),
                  pl.BlockSpec((1, tk, d), cache_idx),
                  pl.BlockSpec((1, tk, d), cache_idx),
                  pl.BlockSpec((sub, sub), lambda b, kb: (0, 0))],
        out_specs=pl.BlockSpec((t, d), lambda b, kb: (b, 0)),
        out_shape=jax.ShapeDtypeStruct((batch * t, d), F32),
        scratch_shapes=[pltpu.VMEM((rows, d), BF16), pltpu.VMEM((rows, d), F32), pltpu.VMEM((rows, 1), F32)],
        compiler_params=_cparams(("arbitrary", "arbitrary")),
        name="sb_sample",
    )(qkv, qkv, qkv, cache_k, cache_v, u)


def _oproj_router_kernel(o_ref, x_ref, wo_ref, g_ref, wr_ref, br_ref, x1_ref, t_ref, gates_ref):
    x1 = x_ref[...] + jnp.dot(o_ref[...].astype(BF16), wo_ref[...], preferred_element_type=F32)
    x1_ref[...] = x1
    ms = jnp.mean(x1 * x1, axis=-1, keepdims=True)
    t = x1 * lax.rsqrt(ms + RMS_EPS) * g_ref[...]
    hi = t.astype(BF16)
    t_ref[...] = hi
    lo = (t - hi.astype(F32)).astype(BF16)
    r_hi = jnp.dot(hi, wr_ref[...], preferred_element_type=F32)
    r_lo = jnp.dot(lo, wr_ref[...], preferred_element_type=F32)
    logits = r_hi + pltpu.roll(r_hi, LANES - ROUTER_LO_LANE, 1) + r_lo + br_ref[...]

    lane = lax.broadcasted_iota(jnp.int32, logits.shape, 1)
    lane_f = lane.astype(F32)
    big = float(LANES)
    is_group = lane < N_GROUPS
    gl = jnp.where(is_group, logits, -jnp.inf)
    gmax = jnp.max(gl, axis=-1, keepdims=True)
    gidx = jnp.min(jnp.where(gl == gmax, lane_f, big), axis=-1, keepdims=True)
    gsum = jnp.sum(jnp.where(is_group, jnp.exp(logits - gmax), 0.0), axis=-1, keepdims=True)
    g_w = 1.0 / gsum
    expert = lane - GATE_LANE0
    in_group = (expert >= 0) & (expert < N_EXPERTS) & ((expert // EXPERTS_PER_GROUP).astype(F32) == gidx)
    el = jnp.where(in_group, logits, -jnp.inf)
    v1 = jnp.max(el, axis=-1, keepdims=True)
    i1 = jnp.min(jnp.where(el == v1, lane_f, big), axis=-1, keepdims=True)
    el2 = jnp.where(lane_f == i1, -jnp.inf, el)
    v2 = jnp.max(el2, axis=-1, keepdims=True)
    i2 = jnp.min(jnp.where(el2 == v2, lane_f, big), axis=-1, keepdims=True)
    e21 = jnp.exp(v2 - v1)
    w1 = g_w / (1.0 + e21)
    w2 = g_w * e21 / (1.0 + e21)
    gates_ref[...] = jnp.where(lane_f == i1, w1, 0.0) + jnp.where(lane_f == i2, w2, 0.0)


def _oproj_router(o, x, wo, g, wr, br):
    n, d = x.shape
    tm = _pick(n, 256)
    row = lambda i: (i, 0)
    const = lambda i: (0, 0)
    return pl.pallas_call(
        _oproj_router_kernel,
        grid=(n // tm,),
        in_specs=[pl.BlockSpec((tm, d), row), pl.BlockSpec((tm, d), row),
                  pl.BlockSpec((d, d), const), pl.BlockSpec((1, d), const),
                  pl.BlockSpec((d, LANES), const), pl.BlockSpec((1, LANES), const)],
        out_specs=[pl.BlockSpec((tm, d), row), pl.BlockSpec((tm, d), row), pl.BlockSpec((tm, LANES), row)],
        out_shape=[jax.ShapeDtypeStruct((n, d), F32), jax.ShapeDtypeStruct((n, d), BF16),
                   jax.ShapeDtypeStruct((n, LANES), F32)],
        compiler_params=_cparams(("arbitrary",)),
        name="oproj_router",
    )(o, x, wo, g.reshape(1, d), wr, br)


def _moe_dense_kernel(t_ref, gates_ref, x1_ref, wg_ref, wu_ref, wd_ref, o_ref, acc_ref):
    e = pl.program_id(1)

    @pl.when(e == 0)
    def _():
        acc_ref[...] = jnp.zeros_like(acc_ref)

    t = t_ref[...]
    hg = jnp.dot(t, wg_ref[0], preferred_element_type=F32)
    hu = jnp.dot(t, wu_ref[0], preferred_element_type=F32)
    hid = (hg * jax.nn.sigmoid(hg) * hu).astype(BF16)
    y = jnp.dot(hid, wd_ref[0], preferred_element_type=F32)
    gates = gates_ref[...]
    lane = lax.broadcasted_iota(jnp.int32, gates.shape, 1)
    gate = jnp.sum(jnp.where(lane == e + GATE_LANE0, gates, 0.0), axis=-1, keepdims=True)
    acc_ref[...] += y * gate

    @pl.when(e == pl.num_programs(1) - 1)
    def _():
        o_ref[...] = x1_ref[...] + acc_ref[...]


def _moe_dense(t, gates, x1, wg, wu, wd):
    n, d = x1.shape
    n_exp, _, de = wg.shape
    tm = _pick(n, 512)
    row = lambda i, e: (i, 0)
    return pl.pallas_call(
        _moe_dense_kernel,
        grid=(n // tm, n_exp),
        in_specs=[pl.BlockSpec((tm, d), row), pl.BlockSpec((tm, LANES), row), pl.BlockSpec((tm, d), row),
                  pl.BlockSpec((1, d, de), lambda i, e: (e, 0, 0)),
                  pl.BlockSpec((1, d, de), lambda i, e: (e, 0, 0)),
                  pl.BlockSpec((1, de, d), lambda i, e: (e, 0, 0))],
        out_specs=pl.BlockSpec((tm, d), row),
        out_shape=jax.ShapeDtypeStruct((n, d), F32),
        scratch_shapes=[pltpu.VMEM((tm, d), F32)],
        compiler_params=_cparams(("arbitrary", "arbitrary")),
        name="moe_dense",
    )(t, gates, x1, wg, wu, wd)


def _rmsnorm_kernel(x_ref, g_ref, o_ref):
    x = x_ref[...]
    ms = jnp.mean(x * x, axis=-1, keepdims=True)
    o_ref[...] = x * lax.rsqrt(ms + RMS_EPS) * g_ref[...]


def _rmsnorm(x, g):
    n, d = x.shape
    tm = _pick(n, 512)
    return pl.pallas_call(
        _rmsnorm_kernel,
        grid=(n // tm,),
        in_specs=[pl.BlockSpec((tm, d), lambda i: (i, 0)), pl.BlockSpec((1, d), lambda i: (0, 0))],
        out_specs=pl.BlockSpec((tm, d), lambda i: (i, 0)),
        out_shape=jax.ShapeDtypeStruct((n, d), F32),
        compiler_params=_cparams(("arbitrary",)),
        name="final_rmsnorm",
    )(x, g.reshape(1, d))


def _rope_tables(pos):
    half = HEAD_DIM_A // 2
    inv = ROPE_THETA ** (-jnp.arange(half, dtype=F32) / half)
    ang = pos.astype(F32)[:, None] * inv[None, :]
    cos = jnp.cos(ang)
    sin = jnp.sin(ang)
    reps = LANES // HEAD_DIM_A
    return jnp.tile(jnp.concatenate([cos, cos], axis=1), (1, reps)), \
        jnp.tile(jnp.concatenate([-sin, sin], axis=1), (1, reps))


def _router_weights(w_group, b_group, w_router, b_router):
    d = w_group.shape[0]
    w = jnp.concatenate([w_group, w_router], axis=1).astype(F32)
    n_log = w.shape[1]
    hi = w.astype(BF16)
    lo = (w - hi.astype(F32)).astype(BF16)
    wr = jnp.zeros((d, LANES), BF16)
    wr = wr.at[:, :n_log].set(hi).at[:, ROUTER_LO_LANE:ROUTER_LO_LANE + n_log].set(lo)
    br = jnp.zeros((1, LANES), F32).at[0, :n_log].set(jnp.concatenate([b_group, b_router]).astype(F32))
    return wr, br


def kernel(x_prompt, x_sample, cache_win_k, cache_win_v, cache_sb_k, cache_sb_v, norm_mix, norm_ffn, norm_final,
           a_w_qkv, a_b_qkv, a_sinks, a_w_o, b_w_qkv, b_w_o, moe_w_group, moe_b_group, moe_w_router,
           moe_b_router, moe_w_gate, moe_w_up, moe_w_down):
    bp, sp, d = x_prompt.shape
    bs, ts, _ = x_sample.shape
    n_p = bp * sp
    n_s = bs * ts
    past = cache_sb_k.shape[2]
    n_kv = cache_win_k.shape[3]
    n_heads_b = cache_sb_k.shape[3]
    nq_a = n_kv * GROUP_A * HEAD_DIM_A
    nk_a = n_kv * HEAD_DIM_A
    assert sp % CHUNK == 0 and past % CHUNK == 0 and ts <= CHUNK and n_p % ts == 0
    assert cache_win_k.shape[2] == WINDOW and d == n_heads_b * HEAD_DIM_B == nq_a

    x = jnp.concatenate([x_prompt.reshape(n_p, d), x_sample.reshape(n_s, d)], axis=0)
    pos = jnp.concatenate([jnp.tile(jnp.arange(sp, dtype=jnp.int32), bp),
                           jnp.tile(past + jnp.arange(ts, dtype=jnp.int32), bs)])
    cos, sin = _rope_tables(pos)
    sub = 256 if sp % 256 == 0 and past % 256 == 0 else 128
    u = (lax.broadcasted_iota(jnp.int32, (sub, sub), 0) > lax.broadcasted_iota(jnp.int32, (sub, sub), 1)).astype(BF16)

    outs = {}
    for i in range(2):
        if i == 0:
            tn = _pick(a_w_qkv.shape[2], 1280)
            qkv = _norm_proj(x, norm_mix[i], a_w_qkv[0].astype(BF16), a_b_qkv[0], cos, sin, nq_a + nk_a, tn)
            sink_rows = jnp.repeat(a_sinks[0].reshape(n_kv, GROUP_A), CHUNK, axis=1)[..., None]
            o_p = _win_prompt(qkv, sink_rows, bp, sp, n_kv)
            sink_rows_s = jnp.repeat(a_sinks[0].reshape(n_kv, GROUP_A), ts, axis=1)[..., None]
            o_s = _win_sample(qkv, cache_win_k[0].reshape(bs, WINDOW, nk_a), cache_win_v[0].reshape(bs, WINDOW, nk_a),
                              sink_rows_s, n_p, bs, ts, n_kv)
            k_all = qkv[:, nq_a:nq_a + nk_a]
            v_all = qkv[:, nq_a + nk_a:]
            outs["wkp"] = k_all[:n_p].reshape(bp, sp, n_kv, HEAD_DIM_A)[:, sp - WINDOW:][None]
            outs["wvp"] = v_all[:n_p].reshape(bp, sp, n_kv, HEAD_DIM_A)[:, sp - WINDOW:][None]
            outs["wks"] = jnp.concatenate(
                [cache_win_k[0], k_all[n_p:].reshape(bs, ts, n_kv, HEAD_DIM_A)], axis=1)[:, -WINDOW:][None]
            outs["wvs"] = jnp.concatenate(
                [cache_win_v[0], v_all[n_p:].reshape(bs, ts, n_kv, HEAD_DIM_A)], axis=1)[:, -WINDOW:][None]
            w_o = a_w_o[0]
        else:
            zeros_b = jnp.zeros((b_w_qkv.shape[2],), F32)
            qkv = _norm_proj(x, norm_mix[i], b_w_qkv[0].astype(BF16), zeros_b, cos, sin, 0, d)
            o_p = _sb_prompt(qkv, u, bp, sp, n_heads_b)
            o_s = _sb_sample(qkv, cache_sb_k[0].reshape(bs, past, d), cache_sb_v[0].reshape(bs, past, d), u,
                             n_p, bs, ts, n_heads_b)
            k_all = qkv[:, d:2 * d]
            v_all = qkv[:, 2 * d:]
            outs["skp"] = k_all[:n_p].reshape(1, bp, sp, n_heads_b, HEAD_DIM_B)
            outs["svp"] = v_all[:n_p].reshape(1, bp, sp, n_heads_b, HEAD_DIM_B)
            outs["sks"] = k_all[n_p:].reshape(1, bs, ts, n_heads_b, HEAD_DIM_B)
            outs["svs"] = v_all[n_p:].reshape(1, bs, ts, n_heads_b, HEAD_DIM_B)
            w_o = b_w_o[0]
        o = jnp.concatenate([o_p, o_s], axis=0)
        wr, br = _router_weights(moe_w_group[i], moe_b_group[i], moe_w_router[i], moe_b_router[i])
        x1, t, gates = _oproj_router(o, x, w_o.astype(BF16), norm_ffn[i], wr, br)
        x = _moe_dense(t, gates, x1, moe_w_gate[i].astype(BF16), moe_w_up[i].astype(BF16),
                       moe_w_down[i].astype(BF16))
    y = _rmsnorm(x, norm_final)
    return (y[:n_p].reshape(bp, sp, d), y[n_p:].reshape(bs, ts, d),
            outs["wkp"], outs["wvp"], outs["wks"], outs["wvs"],
            outs["skp"], outs["svp"], outs["sks"], outs["svs"])
```

```python
import functools

import jax
import jax.numpy as jnp
from jax import lax
from jax.experimental import pallas as pl
from jax.experimental.pallas import tpu as pltpu

F32 = jnp.float32
BF16 = jnp.bfloat16

CHUNK = 64
WINDOW = 128
HEAD_DIM_A = 64
GROUP_A = 8
HEAD_DIM_B = 128
N_GROUPS = 4
EXPERTS_PER_GROUP = 8
N_EXPERTS = N_GROUPS * EXPERTS_PER_GROUP
ROPE_THETA = 10000.0
RMS_EPS = 1e-6
NEG_INF = -1e30

LANES = 128
ROUTER_LO_LANE = 64
GATE_LANE0 = N_GROUPS
VMEM_LIMIT = 56 * 1024 * 1024

NT_DIMS = (((1,), (1,)), ((), ()))


def _cparams(sem):
    return pltpu.CompilerParams(dimension_semantics=sem, vmem_limit_bytes=VMEM_LIMIT)


def _pick(n, pref):
    t = min(n, pref)
    while n % t:
        t //= 2
    return t


def _norm_proj_kernel(x_ref, g_ref, w_ref, b_ref, cos_ref, sin_ref, o_ref, h_ref, *, rope_cols, tn):
    j = pl.program_id(1)

    @pl.when(j == 0)
    def _():
        x = x_ref[...]
        ms = jnp.mean(x * x, axis=-1, keepdims=True)
        h_ref[...] = (x * lax.rsqrt(ms + RMS_EPS) * g_ref[...]).astype(BF16)

    y = jnp.dot(h_ref[...], w_ref[...], preferred_element_type=F32) + b_ref[...]
    if rope_cols:
        reps = tn // LANES
        cos = jnp.concatenate([cos_ref[...]] * reps, axis=1)
        sin = jnp.concatenate([sin_ref[...]] * reps, axis=1)
        lane = lax.broadcasted_iota(jnp.int32, y.shape, 1)
        half = HEAD_DIM_A // 2
        first = (lane % HEAD_DIM_A) < half
        swapped = jnp.where(first, pltpu.roll(y, tn - half, 1), pltpu.roll(y, half, 1))
        roped = y * cos + swapped * sin
        y = jnp.where(lane + j * tn < rope_cols, roped, y)
    o_ref[...] = y


def _norm_proj(x, g, w, b, cos, sin, rope_cols, tn):
    n, d = x.shape
    nout = w.shape[1]
    tm = _pick(n, 512)
    kern = functools.partial(_norm_proj_kernel, rope_cols=rope_cols, tn=tn)
    return pl.pallas_call(
        kern,
        grid=(n // tm, nout // tn),
        in_specs=[
            pl.BlockSpec((tm, d), lambda i, j: (i, 0)),
            pl.BlockSpec((1, d), lambda i, j: (0, 0)),
            pl.BlockSpec((d, tn), lambda i, j: (0, j)),
            pl.BlockSpec((1, tn), lambda i, j: (0, j)),
            pl.BlockSpec((tm, LANES), lambda i, j: (i, 0)),
            pl.BlockSpec((tm, LANES), lambda i, j: (i, 0)),
        ],
        out_specs=pl.BlockSpec((tm, tn), lambda i, j: (i, j)),
        out_shape=jax.ShapeDtypeStruct((n, nout), F32),
        scratch_shapes=[pltpu.VMEM((tm, d), BF16)],
        compiler_params=_cparams(("arbitrary", "arbitrary")),
        name="norm_proj",
    )(x, g.reshape(1, d), w, b.reshape(1, nout), cos, sin)


def _sink_attn_all_groups(q, k, v, sink_ref, valid, n_kv):
    t = q.shape[0]
    hd = HEAD_DIM_A
    scale = hd ** -0.5
    outs = []
    for g in range(n_kv):
        kg = k[:, g * hd:(g + 1) * hd].astype(BF16)
        vg = v[:, g * hd:(g + 1) * hd].astype(BF16)
        qs = jnp.concatenate(
            [q[:, (g * GROUP_A + h) * hd:(g * GROUP_A + h + 1) * hd] for h in range(GROUP_A)], axis=0)
        s = lax.dot_general(qs.astype(BF16), kg, NT_DIMS, preferred_element_type=F32) * scale
        if valid is not None:
            s = jnp.where(valid, s, NEG_INF)
        sink = sink_ref[g]
        m = jnp.maximum(jnp.max(s, axis=-1, keepdims=True), sink)
        p = jnp.exp(s - m)
        denom = jnp.sum(p, axis=-1, keepdims=True) + jnp.exp(sink - m)
        o = jnp.dot(p.astype(BF16), vg, preferred_element_type=F32) / denom
        outs.extend(o[h * t:(h + 1) * t] for h in range(GROUP_A))
    return jnp.concatenate(outs, axis=1)


def _win_prompt_kernel(q_ref, k0_ref, k1_ref, k2_ref, v0_ref, v1_ref, v2_ref, sink_ref, o_ref, *, n_kv):
    c = pl.program_id(1)
    k = jnp.concatenate([k0_ref[...], k1_ref[...], k2_ref[...]], axis=0)
    v = jnp.concatenate([v0_ref[...], v1_ref[...], v2_ref[...]], axis=0)
    col = lax.broadcasted_iota(jnp.int32, (1, 3 * CHUNK), 1)
    valid = col >= (2 - c) * CHUNK
    o_ref[...] = _sink_attn_all_groups(q_ref[...], k, v, sink_ref, valid, n_kv)


def _win_prompt(qkv, sink_rows, batch, seq, n_kv):
    nq = n_kv * GROUP_A * HEAD_DIM_A
    nk = n_kv * HEAD_DIM_A
    nc = seq // CHUNK
    kcol = nq // nk
    q_spec = pl.BlockSpec((CHUNK, nq), lambda b, c: (b * nc + c, 0))

    def kv_spec(back, col):
        return pl.BlockSpec((CHUNK, nk), lambda b, c: (b * nc + jnp.maximum(c - back, 0), col))

    return pl.pallas_call(
        functools.partial(_win_prompt_kernel, n_kv=n_kv),
        grid=(batch, nc),
        in_specs=[q_spec, kv_spec(2, kcol), kv_spec(1, kcol), kv_spec(0, kcol),
                  kv_spec(2, kcol + 1), kv_spec(1, kcol + 1), kv_spec(0, kcol + 1),
                  pl.BlockSpec(sink_rows.shape, lambda b, c: (0, 0, 0))],
        out_specs=pl.BlockSpec((CHUNK, nq), lambda b, c: (b * nc + c, 0)),
        out_shape=jax.ShapeDtypeStruct((batch * seq, nq), F32),
        compiler_params=_cparams(("arbitrary", "arbitrary")),
        name="win_prompt",
    )(qkv, qkv, qkv, qkv, qkv, qkv, qkv, sink_rows)


def _win_sample_kernel(q_ref, kn_ref, vn_ref, ck_ref, cv_ref, sink_ref, o_ref, *, n_kv):
    k = jnp.concatenate([ck_ref[0], kn_ref[...]], axis=0)
    v = jnp.concatenate([cv_ref[0], vn_ref[...]], axis=0)
    o_ref[...] = _sink_attn_all_groups(q_ref[...], k, v, sink_ref, None, n_kv)


def _win_sample(qkv, cache_k, cache_v, sink_rows, row0, batch, t, n_kv):
    nq = n_kv * GROUP_A * HEAD_DIM_A
    nk = n_kv * HEAD_DIM_A
    kcol = nq // nk
    blk0 = row0 // t
    return pl.pallas_call(
        functools.partial(_win_sample_kernel, n_kv=n_kv),
        grid=(batch,),
        in_specs=[pl.BlockSpec((t, nq), lambda b: (blk0 + b, 0)),
                  pl.BlockSpec((t, nk), lambda b: (blk0 + b, kcol)),
                  pl.BlockSpec((t, nk), lambda b: (blk0 + b, kcol + 1)),
                  pl.BlockSpec((1, WINDOW, nk), lambda b: (b, 0, 0)),
                  pl.BlockSpec((1, WINDOW, nk), lambda b: (b, 0, 0)),
                  pl.BlockSpec(sink_rows.shape, lambda b: (0, 0, 0))],
        out_specs=pl.BlockSpec((t, nq), lambda b: (b, 0)),
        out_shape=jax.ShapeDtypeStruct((batch * t, nq), F32),
        compiler_params=_cparams(("arbitrary",)),
        name="win_sample",
    )(qkv, qkv, qkv, cache_k, cache_v, sink_rows)


SB_STOP_BELOW = -110.0
SB_BLOCK = 256


def _sb_weights(z, mask, carry, u):
    sp = jnp.maximum(z, 0.0) + jnp.log1p(jnp.exp(-jnp.abs(z)))
    ls = -sp if mask is None else jnp.where(mask, -sp, 0.0)
    hi = ls.astype(BF16)
    lo = (ls - hi.astype(F32)).astype(BF16)
    after = (jnp.dot(hi, u, preferred_element_type=F32) + jnp.dot(lo, u, preferred_element_type=F32)) + carry
    a = jnp.exp(z - sp + after)
    if mask is not None:
        a = jnp.where(mask, a, 0.0)
    return a, carry + jnp.sum(ls, axis=-1, keepdims=True)


def _sb_prompt_kernel(q_ref, k_ref, v_ref, u_ref, o_ref, *, seq):
    tb = SB_BLOCK
    scale = HEAD_DIM_B ** -0.5
    u = u_ref[...]
    row = lax.broadcasted_iota(jnp.int32, (tb, tb), 0)
    col = lax.broadcasted_iota(jnp.int32, (tb, tb), 1)
    strictly_earlier = col < row

    def scores(q, start):
        kb = k_ref[pl.ds(start, tb), :].astype(BF16)
        return lax.dot_general(q, kb, NT_DIMS, preferred_element_type=F32) * scale

    def weighted_values(a, start):
        vb = v_ref[pl.ds(start, tb), :].astype(BF16)
        return jnp.dot(a.astype(BF16), vb, preferred_element_type=F32)

    def q_block(qi, _):
        q0 = pl.multiple_of(qi * tb, tb)
        q = q_ref[pl.ds(q0, tb), :].astype(BF16)
        a, carry = _sb_weights(scores(q, q0), strictly_earlier, jnp.zeros((tb, 1), F32), u)
        acc = weighted_values(a, q0)

        def more(state):
            kj, cmax, _, _ = state
            return jnp.logical_and(kj >= 0, cmax > SB_STOP_BELOW)

        def earlier_block(state):
            kj, _, carry, acc = state
            k0 = pl.multiple_of(kj * tb, tb)
            a, carry = _sb_weights(scores(q, k0), None, carry, u)
            return kj - 1, jnp.max(carry), carry, acc + weighted_values(a, k0)

        _, _, _, acc = lax.while_loop(more, earlier_block, (qi - 1, jnp.max(carry), carry, acc))
        o_ref[pl.ds(q0, tb), :] = acc
        return 0

    lax.fori_loop(0, seq // tb, q_block, 0)


def _sb_prompt(qkv, u, batch, seq, n_heads):
    hd = HEAD_DIM_B
    assert seq % SB_BLOCK == 0
    return pl.pallas_call(
        functools.partial(_sb_prompt_kernel, seq=seq),
        grid=(batch, n_heads),
        in_specs=[pl.BlockSpec((seq, hd), lambda b, h: (b, h)),
                  pl.BlockSpec((seq, hd), lambda b, h: (b, n_heads + h)),
                  pl.BlockSpec((seq, hd), lambda b, h: (b, 2 * n_heads + h)),
                  pl.BlockSpec((SB_BLOCK, SB_BLOCK), lambda b, h: (0, 0))],
        out_specs=pl.BlockSpec((seq, hd), lambda b, h: (b, h)),
        out_shape=jax.ShapeDtypeStruct((batch * seq, n_heads * hd), F32),
        compiler_params=_cparams(("arbitrary", "arbitrary")),
        name="sb_prompt",
    )(qkv, qkv, qkv, u)


SB_NEW_PAD = SB_BLOCK


def _sb_sample_kernel(q_ref, kn_ref, vn_ref, u_ref, ck_ref, cv_ref, o_ref, kbuf, vbuf, sem,
                      *, t, n_heads, n_blocks):
    b = pl.program_id(0)
    tb = SB_BLOCK
    hd = HEAD_DIM_B
    d = n_heads * hd
    rows = n_heads * t
    scale = hd ** -0.5
    u = u_ref[...]

    def block_copies(j, slot):
        start = pl.multiple_of((n_blocks - 1 - j) * tb, tb)
        out = []
        for h in range(n_heads):
            out.append(pltpu.make_async_copy(ck_ref.at[0, b, pl.ds(start, tb), h, :], kbuf.at[slot, h],
                                             sem.at[slot, 0]))
            out.append(pltpu.make_async_copy(cv_ref.at[0, b, pl.ds(start, tb), h, :], vbuf.at[slot, h],
                                             sem.at[slot, 1]))
        return out

    for c in block_copies(0, 0):
        c.start()

    q = q_ref[...]
    qh = [q[:, h * hd:(h + 1) * hd].astype(BF16) for h in range(n_heads)]

    pad = jnp.zeros((SB_NEW_PAD - t, d), F32)
    kn = jnp.concatenate([kn_ref[...], pad], axis=0).astype(BF16)
    vn = jnp.concatenate([vn_ref[...], pad], axis=0).astype(BF16)
    z = jnp.concatenate(
        [lax.dot_general(qh[h], kn[:, h * hd:(h + 1) * hd], NT_DIMS, preferred_element_type=F32)
         for h in range(n_heads)], axis=0) * scale
    rr = lax.broadcasted_iota(jnp.int32, (rows, SB_NEW_PAD), 0) % t
    cc = lax.broadcasted_iota(jnp.int32, (rows, SB_NEW_PAD), 1)
    a, carry = _sb_weights(z, cc < rr, jnp.zeros((rows, 1), F32), u[:SB_NEW_PAD, :SB_NEW_PAD])
    a = a.astype(BF16)
    acc = jnp.concatenate(
        [jnp.dot(a[h * t:(h + 1) * t], vn[:, h * hd:(h + 1) * hd], preferred_element_type=F32)
         for h in range(n_heads)], axis=1)

    def more(state):
        j, cmax, _, _ = state
        return jnp.logical_and(j < n_blocks, cmax > SB_STOP_BELOW)

    def cache_block(state):
        j, _, carry, acc = state
        slot = j % 2
        for c in block_copies(j, slot):
            c.wait()

        @pl.when(j + 1 < n_blocks)
        def _():
            for c in block_copies(j + 1, 1 - slot):
                c.start()

        z = jnp.concatenate(
            [lax.dot_general(qh[h], kbuf[slot, h].astype(BF16), NT_DIMS, preferred_element_type=F32)
             for h in range(n_heads)], axis=0) * scale
        a, carry = _sb_weights(z, None, carry, u)
        a = a.astype(BF16)
        out = jnp.concatenate(
            [jnp.dot(a[h * t:(h + 1) * t], vbuf[slot, h].astype(BF16), preferred_element_type=F32)
             for h in range(n_heads)], axis=1)
        return j + 1, jnp.max(carry), carry, acc + out

    j, _, _, acc = lax.while_loop(more, cache_block, (jnp.int32(0), jnp.max(carry), carry, acc))
    o_ref[...] = acc

    @pl.when(j < n_blocks)
    def _():
        for c in block_copies(j, j % 2):
            c.wait()


def _sb_sample(qkv, cache_k, cache_v, u, row0, batch, t, n_heads):
    hd = HEAD_DIM_B
    d = n_heads * hd
    past = cache_k.shape[2]
    assert past % SB_BLOCK == 0 and t <= SB_NEW_PAD
    blk0 = row0 // t
    return pl.pallas_call(
        functools.partial(_sb_sample_kernel, t=t, n_heads=n_heads, n_blocks=past // SB_BLOCK),
        grid=(batch,),
        in_specs=[pl.BlockSpec((t, d), lambda b: (blk0 + b, 0)),
                  pl.BlockSpec((t, d), lambda b: (blk0 + b, 1)),
                  pl.BlockSpec((t, d), lambda b: (blk0 + b, 2)),
                  pl.BlockSpec((SB_BLOCK, SB_BLOCK), lambda b: (0, 0)),
                  pl.BlockSpec(memory_space=pl.ANY),
                  pl.BlockSpec(memory_space=pl.ANY)],
        out_specs=pl.BlockSpec((t, d), lambda b: (b, 0)),
        out_shape=jax.ShapeDtypeStruct((batch * t, d), F32),
        scratch_shapes=[pltpu.VMEM((2, n_heads, SB_BLOCK, hd), F32),
                        pltpu.VMEM((2, n_heads, SB_BLOCK, hd), F32),
                        pltpu.SemaphoreType.DMA((2, 2))],
        compiler_params=_cparams(("arbitrary",)),
        name="sb_sample",
    )(qkv, qkv, qkv, u, cache_k, cache_v)


def _pack_bf16_pair(a, b):
    ab = lax.bitcast_convert_type(a.astype(BF16).astype(F32), jnp.uint32)
    bb = lax.bitcast_convert_type(b.astype(BF16).astype(F32), jnp.uint32)
    return (ab >> 16) | (bb & jnp.uint32(0xFFFF0000))


def _unpack_bf16_pair(p):
    a = lax.bitcast_convert_type(p << 16, F32)
    b = lax.bitcast_convert_type(p & jnp.uint32(0xFFFF0000), F32)
    return a.astype(BF16), b.astype(BF16)


def _oproj_router_kernel(o_ref, x_ref, wo_ref, g_ref, wr_ref, br_ref, x1_ref, tp_ref, route_ref):
    x1 = x_ref[...] + jnp.dot(o_ref[...].astype(BF16), wo_ref[...], preferred_element_type=F32)
    x1_ref[...] = x1
    ms = jnp.mean(x1 * x1, axis=-1, keepdims=True)
    t = x1 * lax.rsqrt(ms + RMS_EPS) * g_ref[...]
    half = t.shape[1] // 2
    tp_ref[...] = _pack_bf16_pair(t[:, :half], t[:, half:])
    hi = t.astype(BF16)
    lo = (t - hi.astype(F32)).astype(BF16)
    r_hi = jnp.dot(hi, wr_ref[...], preferred_element_type=F32)
    r_lo = jnp.dot(lo, wr_ref[...], preferred_element_type=F32)
    logits = r_hi + pltpu.roll(r_hi, LANES - ROUTER_LO_LANE, 1) + r_lo + br_ref[...]

    lane = lax.broadcasted_iota(jnp.int32, logits.shape, 1)
    lane_f = lane.astype(F32)
    big = float(LANES)
    is_group = lane < N_GROUPS
    gl = jnp.where(is_group, logits, -jnp.inf)
    gmax = jnp.max(gl, axis=-1, keepdims=True)
    gidx = jnp.min(jnp.where(gl == gmax, lane_f, big), axis=-1, keepdims=True)
    gsum = jnp.sum(jnp.where(is_group, jnp.exp(logits - gmax), 0.0), axis=-1, keepdims=True)
    g_w = 1.0 / gsum
    expert = lane - GATE_LANE0
    in_group = (expert >= 0) & (expert < N_EXPERTS) & ((expert // EXPERTS_PER_GROUP).astype(F32) == gidx)
    el = jnp.where(in_group, logits, -jnp.inf)
    v1 = jnp.max(el, axis=-1, keepdims=True)
    i1 = jnp.min(jnp.where(el == v1, lane_f, big), axis=-1, keepdims=True)
    el2 = jnp.where(lane_f == i1, -jnp.inf, el)
    v2 = jnp.max(el2, axis=-1, keepdims=True)
    i2 = jnp.min(jnp.where(el2 == v2, lane_f, big), axis=-1, keepdims=True)
    e21 = jnp.exp(v2 - v1)
    w1 = g_w / (1.0 + e21)
    w2 = g_w * e21 / (1.0 + e21)
    route_ref[...] = jnp.where(lane == 0, i1 - GATE_LANE0,
                               jnp.where(lane == 1, i2 - GATE_LANE0,
                                         jnp.where(lane == 2, w1, jnp.where(lane == 3, w2, 0.0))))


def _oproj_router(o, x, wo, g, wr, br):
    n, d = x.shape
    tm = _pick(n, 256)
    row = lambda i: (i, 0)
    const = lambda i: (0, 0)
    return pl.pallas_call(
        _oproj_router_kernel,
        grid=(n // tm,),
        in_specs=[pl.BlockSpec((tm, d), row), pl.BlockSpec((tm, d), row),
                  pl.BlockSpec((d, d), const), pl.BlockSpec((1, d), const),
                  pl.BlockSpec((d, LANES), const), pl.BlockSpec((1, LANES), const)],
        out_specs=[pl.BlockSpec((tm, d), row), pl.BlockSpec((tm, d // 2), row), pl.BlockSpec((tm, LANES), row)],
        out_shape=[jax.ShapeDtypeStruct((n, d), F32), jax.ShapeDtypeStruct((n, d // 2), jnp.uint32),
                   jax.ShapeDtypeStruct((n, LANES), F32)],
        compiler_params=_cparams(("arbitrary",)),
        name="oproj_router",
    )(o, x, wo, g.reshape(1, d), wr, br)


MOE_TILE = 256
TOP_K = 2


def _moe_plan(route, n_tiles):
    n = route.shape[0]
    e_flat = route[:, :TOP_K].astype(jnp.int32).reshape(n * TOP_K)
    onehot = (e_flat[:, None] == jnp.arange(N_EXPERTS, dtype=jnp.int32)[None, :]).astype(jnp.int32)
    csum = jnp.cumsum(onehot, axis=0)
    rank = jnp.sum(csum * onehot, axis=1) - 1
    counts = csum[-1]
    tiles = (counts + MOE_TILE - 1) // MOE_TILE
    tile_end = jnp.cumsum(tiles)
    row_start = (tile_end - tiles) * MOE_TILE
    pos = jnp.sum(onehot * row_start[None, :], axis=1) + rank
    n_used = tile_end[-1]
    tile_id = jnp.minimum(jnp.arange(n_tiles, dtype=jnp.int32), n_used - 1)
    tile_expert = jnp.sum((tile_end[None, :] <= tile_id[:, None]).astype(jnp.int32), axis=1)
    return pos.astype(jnp.int32), tile_expert.astype(jnp.int32), n_used.reshape(1).astype(jnp.int32)


def _dispatch_kernel(pos_ref, tp_ref, xs_in_ref, xs_ref, sem, *, tc):
    del xs_in_ref
    i = pl.program_id(0)
    base = i * tc

    def issue(j, carry):
        tok = base + j
        for s in range(TOP_K):
            p = pos_ref[TOP_K * tok + s]
            pltpu.make_async_copy(tp_ref.at[pl.ds(tok, 1)], xs_ref.at[pl.ds(p, 1)], sem).start()
        return carry

    lax.fori_loop(0, tc, issue, 0)

    def wait_chunk():
        pltpu.make_async_copy(tp_ref.at[pl.ds(0, TOP_K * tc)], xs_ref.at[pl.ds(0, TOP_K * tc)], sem).wait()

    @pl.when(i > 0)
    def _():
        wait_chunk()

    @pl.when(i == pl.num_programs(0) - 1)
    def _():
        wait_chunk()


def _dispatch(pos, tp, n_rows_pad):
    n, dw = tp.shape
    tc = _pick(n, 512)
    assert TOP_K * tc <= n
    grid_spec = pltpu.PrefetchScalarGridSpec(
        num_scalar_prefetch=1, grid=(n // tc,),
        in_specs=[pl.BlockSpec(memory_space=pl.ANY), pl.BlockSpec(memory_space=pl.ANY)],
        out_specs=pl.BlockSpec(memory_space=pl.ANY),
        scratch_shapes=[pltpu.SemaphoreType.DMA(())])
    return pl.pallas_call(
        functools.partial(_dispatch_kernel, tc=tc),
        grid_spec=grid_spec,
        out_shape=jax.ShapeDtypeStruct((n_rows_pad, dw), jnp.uint32),
        input_output_aliases={2: 0},
        compiler_params=_cparams(("arbitrary",)),
        name="moe_dispatch",
    )(pos, tp, jnp.zeros((n_rows_pad, dw), jnp.uint32))


def _experts_kernel(te_ref, nu_ref, xs_ref, wg_ref, wu_ref, wd_ref, ys_ref):
    del te_ref
    used = pl.program_id(0) < nu_ref[0]

    @pl.when(jnp.logical_not(used))
    def _():
        ys_ref[...] = jnp.zeros_like(ys_ref)

    @pl.when(used)
    def _():
        a, b = _unpack_bf16_pair(xs_ref[...])
        x = jnp.concatenate([a, b], axis=1)
        hg = jnp.dot(x, wg_ref[0], preferred_element_type=F32)
        hu = jnp.dot(x, wu_ref[0], preferred_element_type=F32)
        hid = (hg * jax.nn.sigmoid(hg) * hu).astype(BF16)
        ys_ref[...] = jnp.dot(hid, wd_ref[0], preferred_element_type=F32)


def _experts(tile_expert, n_used, xs, wg, wu, wd):
    rows, dw = xs.shape
    _, d, de = wg.shape
    grid_spec = pltpu.PrefetchScalarGridSpec(
        num_scalar_prefetch=2, grid=(rows // MOE_TILE,),
        in_specs=[pl.BlockSpec((MOE_TILE, dw), lambda i, te, nu: (i, 0)),
                  pl.BlockSpec((1, d, de), lambda i, te, nu: (te[i], 0, 0)),
                  pl.BlockSpec((1, d, de), lambda i, te, nu: (te[i], 0, 0)),
                  pl.BlockSpec((1, de, d), lambda i, te, nu: (te[i], 0, 0))],
        out_specs=pl.BlockSpec((MOE_TILE, d), lambda i, te, nu: (i, 0)))
    return pl.pallas_call(
        _experts_kernel,
        grid_spec=grid_spec,
        out_shape=jax.ShapeDtypeStruct((rows, d), F32),
        compiler_params=_cparams(("arbitrary",)),
        name="moe_experts",
    )(tile_expert, n_used, xs, wg, wu, wd)


def _combine_kernel(pos_ref, ys_ref, x1_ref, route_ref, o_ref, buf_ref, sem, *, tc):
    i = pl.program_id(0)
    n_steps = pl.num_programs(0)

    def issue(step, slot):
        def body(j, carry):
            tok = step * tc + j
            for s in range(TOP_K):
                p = pos_ref[TOP_K * tok + s]
                pltpu.make_async_copy(ys_ref.at[pl.ds(p, 1)], buf_ref.at[slot, s, pl.ds(j, 1)],
                                      sem.at[slot]).start()
            return carry
        lax.fori_loop(0, tc, body, 0)

    @pl.when(i == 0)
    def _():
        issue(0, 0)

    @pl.when(i + 1 < n_steps)
    def _():
        issue(i + 1, (i + 1) % 2)

    slot = i % 2
    for s in range(TOP_K):
        pltpu.make_async_copy(ys_ref.at[pl.ds(0, tc)], buf_ref.at[slot, s], sem.at[slot]).wait()
    route = route_ref[...]
    w0 = route[:, TOP_K:TOP_K + 1]
    w1 = route[:, TOP_K + 1:TOP_K + 2]
    o_ref[...] = x1_ref[...] + buf_ref[slot, 0] * w0 + buf_ref[slot, 1] * w1


def _combine(pos, ys, x1, route):
    n, d = x1.shape
    tc = _pick(n, 256)
    grid_spec = pltpu.PrefetchScalarGridSpec(
        num_scalar_prefetch=1, grid=(n // tc,),
        in_specs=[pl.BlockSpec(memory_space=pl.ANY),
                  pl.BlockSpec((tc, d), lambda i, pos: (i, 0)),
                  pl.BlockSpec((tc, LANES), lambda i, pos: (i, 0))],
        out_specs=pl.BlockSpec((tc, d), lambda i, pos: (i, 0)),
        scratch_shapes=[pltpu.VMEM((2, TOP_K, tc, d), F32), pltpu.SemaphoreType.DMA((2,))])
    return pl.pallas_call(
        functools.partial(_combine_kernel, tc=tc),
        grid_spec=grid_spec,
        out_shape=jax.ShapeDtypeStruct((n, d), F32),
        compiler_params=_cparams(("arbitrary",)),
        name="moe_combine",
    )(pos, ys, x1, route)


def _moe(tp, route, x1, wg, wu, wd):
    n = x1.shape[0]
    n_tiles = -(-n * TOP_K // MOE_TILE) + N_EXPERTS
    pos, tile_expert, n_used = _moe_plan(route, n_tiles)
    xs = _dispatch(pos, tp, n_tiles * MOE_TILE)
    ys = _experts(tile_expert, n_used, xs, wg, wu, wd)
    return _combine(pos, ys, x1, route)


def _rmsnorm_kernel(x_ref, g_ref, o_ref):
    x = x_ref[...]
    ms = jnp.mean(x * x, axis=-1, keepdims=True)
    o_ref[...] = x * lax.rsqrt(ms + RMS_EPS) * g_ref[...]


def _rmsnorm(x, g):
    n, d = x.shape
    tm = _pick(n, 512)
    return pl.pallas_call(
        _rmsnorm_kernel,
        grid=(n // tm,),
        in_specs=[pl.BlockSpec((tm, d), lambda i: (i, 0)), pl.BlockSpec((1, d), lambda i: (0, 0))],
        out_specs=pl.BlockSpec((tm, d), lambda i: (i, 0)),
        out_shape=jax.ShapeDtypeStruct((n, d), F32),
        compiler_params=_cparams(("arbitrary",)),
        name="final_rmsnorm",
    )(x, g.reshape(1, d))


def _rope_tables(pos):
    half = HEAD_DIM_A // 2
    inv = ROPE_THETA ** (-jnp.arange(half, dtype=F32) / half)
    ang = pos.astype(F32)[:, None] * inv[None, :]
    cos = jnp.cos(ang)
    sin = jnp.sin(ang)
    reps = LANES // HEAD_DIM_A
    return jnp.tile(jnp.concatenate([cos, cos], axis=1), (1, reps)), \
        jnp.tile(jnp.concatenate([-sin, sin], axis=1), (1, reps))


def _router_weights(w_group, b_group, w_router, b_router):
    d = w_group.shape[0]
    w = jnp.concatenate([w_group, w_router], axis=1).astype(F32)
    n_log = w.shape[1]
    hi = w.astype(BF16)
    lo = (w - hi.astype(F32)).astype(BF16)
    wr = jnp.zeros((d, LANES), BF16)
    wr = wr.at[:, :n_log].set(hi).at[:, ROUTER_LO_LANE:ROUTER_LO_LANE + n_log].set(lo)
    br = jnp.zeros((1, LANES), F32).at[0, :n_log].set(jnp.concatenate([b_group, b_router]).astype(F32))
    return wr, br


def kernel(x_prompt, x_sample, cache_win_k, cache_win_v, cache_sb_k, cache_sb_v, norm_mix, norm_ffn, norm_final,
           a_w_qkv, a_b_qkv, a_sinks, a_w_o, b_w_qkv, b_w_o, moe_w_group, moe_b_group, moe_w_router,
           moe_b_router, moe_w_gate, moe_w_up, moe_w_down):
    bp, sp, d = x_prompt.shape
    bs, ts, _ = x_sample.shape
    n_p = bp * sp
    n_s = bs * ts
    past = cache_sb_k.shape[2]
    n_kv = cache_win_k.shape[3]
    n_heads_b = cache_sb_k.shape[3]
    nq_a = n_kv * GROUP_A * HEAD_DIM_A
    nk_a = n_kv * HEAD_DIM_A
    assert sp % CHUNK == 0 and past % CHUNK == 0 and ts <= CHUNK and n_p % ts == 0
    assert cache_win_k.shape[2] == WINDOW and d == n_heads_b * HEAD_DIM_B == nq_a

    x = jnp.concatenate([x_prompt.reshape(n_p, d), x_sample.reshape(n_s, d)], axis=0)
    pos = jnp.concatenate([jnp.tile(jnp.arange(sp, dtype=jnp.int32), bp),
                           jnp.tile(past + jnp.arange(ts, dtype=jnp.int32), bs)])
    cos, sin = _rope_tables(pos)
    u = (lax.broadcasted_iota(jnp.int32, (SB_BLOCK, SB_BLOCK), 0)
         > lax.broadcasted_iota(jnp.int32, (SB_BLOCK, SB_BLOCK), 1)).astype(BF16)

    outs = {}
    for i in range(2):
        if i == 0:
            tn = _pick(a_w_qkv.shape[2], 1280)
            qkv = _norm_proj(x, norm_mix[i], a_w_qkv[0].astype(BF16), a_b_qkv[0], cos, sin, nq_a + nk_a, tn)
            sink_rows = jnp.repeat(a_sinks[0].reshape(n_kv, GROUP_A), CHUNK, axis=1)[..., None]
            o_p = _win_prompt(qkv, sink_rows, bp, sp, n_kv)
            sink_rows_s = jnp.repeat(a_sinks[0].reshape(n_kv, GROUP_A), ts, axis=1)[..., None]
            o_s = _win_sample(qkv, cache_win_k[0].reshape(bs, WINDOW, nk_a), cache_win_v[0].reshape(bs, WINDOW, nk_a),
                              sink_rows_s, n_p, bs, ts, n_kv)
            k_all = qkv[:, nq_a:nq_a + nk_a]
            v_all = qkv[:, nq_a + nk_a:]
            outs["wkp"] = k_all[:n_p].reshape(bp, sp, n_kv, HEAD_DIM_A)[:, sp - WINDOW:][None]
            outs["wvp"] = v_all[:n_p].reshape(bp, sp, n_kv, HEAD_DIM_A)[:, sp - WINDOW:][None]
            outs["wks"] = jnp.concatenate(
                [cache_win_k[0], k_all[n_p:].reshape(bs, ts, n_kv, HEAD_DIM_A)], axis=1)[:, -WINDOW:][None]
            outs["wvs"] = jnp.concatenate(
                [cache_win_v[0], v_all[n_p:].reshape(bs, ts, n_kv, HEAD_DIM_A)], axis=1)[:, -WINDOW:][None]
            w_o = a_w_o[0]
        else:
            zeros_b = jnp.zeros((b_w_qkv.shape[2],), F32)
            qkv = _norm_proj(x, norm_mix[i], b_w_qkv[0].astype(BF16), zeros_b, cos, sin, 0, d)
            o_p = _sb_prompt(qkv, u, bp, sp, n_heads_b)
            o_s = _sb_sample(qkv, cache_sb_k, cache_sb_v, u, n_p, bs, ts, n_heads_b)
            k_all = qkv[:, d:2 * d]
            v_all = qkv[:, 2 * d:]
            outs["skp"] = k_all[:n_p].reshape(1, bp, sp, n_heads_b, HEAD_DIM_B)
            outs["svp"] = v_all[:n_p].reshape(1, bp, sp, n_heads_b, HEAD_DIM_B)
            outs["sks"] = k_all[n_p:].reshape(1, bs, ts, n_heads_b, HEAD_DIM_B)
            outs["svs"] = v_all[n_p:].reshape(1, bs, ts, n_heads_b, HEAD_DIM_B)
            w_o = b_w_o[0]
        o = jnp.concatenate([o_p, o_s], axis=0)
        wr, br = _router_weights(moe_w_group[i], moe_b_group[i], moe_w_router[i], moe_b_router[i])
        x1, tp, route = _oproj_router(o, x, w_o.astype(BF16), norm_ffn[i], wr, br)
        x = _moe(tp, route, x1, moe_w_gate[i].astype(BF16), moe_w_up[i].astype(BF16),
                 moe_w_down[i].astype(BF16))
    y = _rmsnorm(x, norm_final)
    return (y[:n_p].reshape(bp, sp, d), y[n_p:].reshape(bs, ts, d),
            outs["wkp"], outs["wvp"], outs["wks"], outs["wvs"],
            outs["skp"], outs["svp"], outs["sks"], outs["svs"])
```

```python
import functools
import math

import jax
import jax.numpy as jnp
from jax import lax
from jax.experimental import pallas as pl
from jax.experimental.pallas import tpu as pltpu

F32 = jnp.float32
BF16 = jnp.bfloat16

CHUNK = 64
WINDOW = 128
HEAD_DIM_A = 64
GROUP_A = 8
HEAD_DIM_B = 128
N_GROUPS = 4
EXPERTS_PER_GROUP = 8
N_EXPERTS = N_GROUPS * EXPERTS_PER_GROUP
ROPE_THETA = 10000.0
RMS_EPS = 1e-6
NEG_INF = -1e30

LANES = 128
ROUTER_LO_LANE = 64
GATE_LANE0 = N_GROUPS
VMEM_LIMIT = 56 * 1024 * 1024

NT_DIMS = (((1,), (1,)), ((), ()))


def _cparams(sem):
    return pltpu.CompilerParams(dimension_semantics=sem, vmem_limit_bytes=VMEM_LIMIT)


def _pick(n, pref, align=8):
    for t in range(min(n, pref), 0, -1):
        if n % t == 0 and t % align == 0:
            return t
    raise ValueError((n, pref, align))


def _norm_proj_kernel(x_ref, g_ref, w_ref, b_ref, cos_ref, sin_ref, o_ref, h_ref, *, rope_cols, tn):
    j = pl.program_id(1)

    @pl.when(j == 0)
    def _():
        x = x_ref[...]
        ms = jnp.mean(x * x, axis=-1, keepdims=True)
        h_ref[...] = (x * lax.rsqrt(ms + RMS_EPS) * g_ref[...]).astype(BF16)

    y = jnp.dot(h_ref[...], w_ref[...], preferred_element_type=F32) + b_ref[...]
    if rope_cols:
        reps = tn // LANES
        cos = jnp.concatenate([cos_ref[...]] * reps, axis=1)
        sin = jnp.concatenate([sin_ref[...]] * reps, axis=1)
        lane = lax.broadcasted_iota(jnp.int32, y.shape, 1)
        half = HEAD_DIM_A // 2
        first = (lane % HEAD_DIM_A) < half
        swapped = jnp.where(first, pltpu.roll(y, tn - half, 1), pltpu.roll(y, half, 1))
        roped = y * cos + swapped * sin
        y = jnp.where(lane + j * tn < rope_cols, roped, y)
    o_ref[...] = y


def _norm_proj(x, g, w, b, cos, sin, rope_cols, tn):
    n, d = x.shape
    nout = w.shape[1]
    tm = _pick(n, 512)
    kern = functools.partial(_norm_proj_kernel, rope_cols=rope_cols, tn=tn)
    return pl.pallas_call(
        kern,
        grid=(n // tm, nout // tn),
        in_specs=[
            pl.BlockSpec((tm, d), lambda i, j: (i, 0)),
            pl.BlockSpec((1, d), lambda i, j: (0, 0)),
            pl.BlockSpec((d, tn), lambda i, j: (0, j)),
            pl.BlockSpec((1, tn), lambda i, j: (0, j)),
            pl.BlockSpec((tm, LANES), lambda i, j: (i, 0)),
            pl.BlockSpec((tm, LANES), lambda i, j: (i, 0)),
        ],
        out_specs=pl.BlockSpec((tm, tn), lambda i, j: (i, j)),
        out_shape=jax.ShapeDtypeStruct((n, nout), F32),
        scratch_shapes=[pltpu.VMEM((tm, d), BF16)],
        compiler_params=_cparams(("arbitrary", "arbitrary")),
        name="norm_proj",
    )(x, g.reshape(1, d), w, b.reshape(1, nout), cos, sin)


def _sink_attn_all_groups(q, k, v, sink_ref, valid, n_kv):
    t = q.shape[0]
    hd = HEAD_DIM_A
    scale = hd ** -0.5
    outs = []
    for g in range(n_kv):
        kg = k[:, g * hd:(g + 1) * hd].astype(BF16)
        vg = v[:, g * hd:(g + 1) * hd].astype(BF16)
        qs = jnp.concatenate(
            [q[:, (g * GROUP_A + h) * hd:(g * GROUP_A + h + 1) * hd] for h in range(GROUP_A)], axis=0)
        s = lax.dot_general(qs.astype(BF16), kg, NT_DIMS, preferred_element_type=F32) * scale
        if valid is not None:
            s = jnp.where(valid, s, NEG_INF)
        sink = sink_ref[g]
        m = jnp.maximum(jnp.max(s, axis=-1, keepdims=True), sink)
        p = jnp.exp(s - m)
        denom = jnp.sum(p, axis=-1, keepdims=True) + jnp.exp(sink - m)
        o = jnp.dot(p.astype(BF16), vg, preferred_element_type=F32) / denom
        outs.extend(o[h * t:(h + 1) * t] for h in range(GROUP_A))
    return jnp.concatenate(outs, axis=1)


def _win_prompt_kernel(q_ref, k0_ref, k1_ref, k2_ref, v0_ref, v1_ref, v2_ref, sink_ref, o_ref, *, n_kv):
    c = pl.program_id(1)
    k = jnp.concatenate([k0_ref[...], k1_ref[...], k2_ref[...]], axis=0)
    v = jnp.concatenate([v0_ref[...], v1_ref[...], v2_ref[...]], axis=0)
    col = lax.broadcasted_iota(jnp.int32, (1, 3 * CHUNK), 1)
    valid = col >= (2 - c) * CHUNK
    o_ref[...] = _sink_attn_all_groups(q_ref[...], k, v, sink_ref, valid, n_kv).astype(o_ref.dtype)


def _win_prompt(qkv, sink_rows, batch, seq, n_kv):
    nq = n_kv * GROUP_A * HEAD_DIM_A
    nk = n_kv * HEAD_DIM_A
    nc = seq // CHUNK
    kcol = nq // nk
    q_spec = pl.BlockSpec((CHUNK, nq), lambda b, c: (b * nc + c, 0))

    def kv_spec(back, col):
        return pl.BlockSpec((CHUNK, nk), lambda b, c: (b * nc + jnp.maximum(c - back, 0), col))

    return pl.pallas_call(
        functools.partial(_win_prompt_kernel, n_kv=n_kv),
        grid=(batch, nc),
        in_specs=[q_spec, kv_spec(2, kcol), kv_spec(1, kcol), kv_spec(0, kcol),
                  kv_spec(2, kcol + 1), kv_spec(1, kcol + 1), kv_spec(0, kcol + 1),
                  pl.BlockSpec(sink_rows.shape, lambda b, c: (0, 0, 0))],
        out_specs=pl.BlockSpec((CHUNK, nq), lambda b, c: (b * nc + c, 0)),
        out_shape=jax.ShapeDtypeStruct((batch * seq, nq), BF16),
        compiler_params=_cparams(("arbitrary", "arbitrary")),
        name="win_prompt",
    )(qkv, qkv, qkv, qkv, qkv, qkv, qkv, sink_rows)


def _win_sample_kernel(q_ref, kn_ref, vn_ref, ck_ref, cv_ref, sink_ref, o_ref, *, n_kv):
    k = jnp.concatenate([ck_ref[0], kn_ref[...]], axis=0)
    v = jnp.concatenate([cv_ref[0], vn_ref[...]], axis=0)
    o_ref[...] = _sink_attn_all_groups(q_ref[...], k, v, sink_ref, None, n_kv).astype(o_ref.dtype)


def _win_sample(qkv, cache_k, cache_v, sink_rows, row0, batch, t, n_kv):
    nq = n_kv * GROUP_A * HEAD_DIM_A
    nk = n_kv * HEAD_DIM_A
    kcol = nq // nk
    blk0 = row0 // t
    return pl.pallas_call(
        functools.partial(_win_sample_kernel, n_kv=n_kv),
        grid=(batch,),
        in_specs=[pl.BlockSpec((t, nq), lambda b: (blk0 + b, 0)),
                  pl.BlockSpec((t, nk), lambda b: (blk0 + b, kcol)),
                  pl.BlockSpec((t, nk), lambda b: (blk0 + b, kcol + 1)),
                  pl.BlockSpec((1, WINDOW, nk), lambda b: (b, 0, 0)),
                  pl.BlockSpec((1, WINDOW, nk), lambda b: (b, 0, 0)),
                  pl.BlockSpec(sink_rows.shape, lambda b: (0, 0, 0))],
        out_specs=pl.BlockSpec((t, nq), lambda b: (b, 0)),
        out_shape=jax.ShapeDtypeStruct((batch * t, nq), BF16),
        compiler_params=_cparams(("arbitrary",)),
        name="win_sample",
    )(qkv, qkv, qkv, cache_k, cache_v, sink_rows)


SB_STOP_BELOW = -110.0
SB_BLOCK = 256


def _sb_weights(z, mask, carry, u):
    sp = jnp.maximum(z, 0.0) + jnp.log1p(jnp.exp(-jnp.abs(z)))
    ls = -sp if mask is None else jnp.where(mask, -sp, 0.0)
    hi = ls.astype(BF16)
    lo = (ls - hi.astype(F32)).astype(BF16)
    after = (jnp.dot(hi, u, preferred_element_type=F32) + jnp.dot(lo, u, preferred_element_type=F32)) + carry
    a = jnp.exp(z - sp + after)
    if mask is not None:
        a = jnp.where(mask, a, 0.0)
    return a, carry + jnp.sum(ls, axis=-1, keepdims=True)


def _sb_prompt_kernel(q_ref, k_ref, v_ref, u_ref, o_ref, *, seq):
    tb = SB_BLOCK
    scale = HEAD_DIM_B ** -0.5
    u = u_ref[...]
    row = lax.broadcasted_iota(jnp.int32, (tb, tb), 0)
    col = lax.broadcasted_iota(jnp.int32, (tb, tb), 1)
    strictly_earlier = col < row

    def scores(q, start):
        kb = k_ref[pl.ds(start, tb), :].astype(BF16)
        return lax.dot_general(q, kb, NT_DIMS, preferred_element_type=F32) * scale

    def weighted_values(a, start):
        vb = v_ref[pl.ds(start, tb), :].astype(BF16)
        return jnp.dot(a.astype(BF16), vb, preferred_element_type=F32)

    def q_block(qi, _):
        q0 = pl.multiple_of(qi * tb, tb)
        q = q_ref[pl.ds(q0, tb), :].astype(BF16)
        a, carry = _sb_weights(scores(q, q0), strictly_earlier, jnp.zeros((tb, 1), F32), u)
        acc = weighted_values(a, q0)

        def more(state):
            kj, cmax, _, _ = state
            return jnp.logical_and(kj >= 0, cmax > SB_STOP_BELOW)

        def earlier_block(state):
            kj, _, carry, acc = state
            k0 = pl.multiple_of(kj * tb, tb)
            a, carry = _sb_weights(scores(q, k0), None, carry, u)
            return kj - 1, jnp.max(carry), carry, acc + weighted_values(a, k0)

        _, _, _, acc = lax.while_loop(more, earlier_block, (qi - 1, jnp.max(carry), carry, acc))
        o_ref[pl.ds(q0, tb), :] = acc.astype(o_ref.dtype)
        return 0

    lax.fori_loop(0, seq // tb, q_block, 0)


def _sb_prompt(qkv, u, batch, seq, n_heads):
    hd = HEAD_DIM_B
    assert seq % SB_BLOCK == 0
    return pl.pallas_call(
        functools.partial(_sb_prompt_kernel, seq=seq),
        grid=(batch, n_heads),
        in_specs=[pl.BlockSpec((seq, hd), lambda b, h: (b, h)),
                  pl.BlockSpec((seq, hd), lambda b, h: (b, n_heads + h)),
                  pl.BlockSpec((seq, hd), lambda b, h: (b, 2 * n_heads + h)),
                  pl.BlockSpec((SB_BLOCK, SB_BLOCK), lambda b, h: (0, 0))],
        out_specs=pl.BlockSpec((seq, hd), lambda b, h: (b, h)),
        out_shape=jax.ShapeDtypeStruct((batch * seq, n_heads * hd), BF16),
        compiler_params=_cparams(("arbitrary", "arbitrary")),
        name="sb_prompt",
    )(qkv, qkv, qkv, u)


SB_NEW_PAD = SB_BLOCK


def _sb_sample_kernel(q_ref, kn_ref, vn_ref, u_ref, ck_ref, cv_ref, o_ref, kbuf, vbuf, sem,
                      *, t, n_heads, n_blocks):
    b = pl.program_id(0)
    tb = SB_BLOCK
    hd = HEAD_DIM_B
    d = n_heads * hd
    rows = n_heads * t
    scale = hd ** -0.5
    u = u_ref[...]

    def block_copies(j, slot):
        start = pl.multiple_of((n_blocks - 1 - j) * tb, tb)
        out = []
        for h in range(n_heads):
            out.append(pltpu.make_async_copy(ck_ref.at[0, b, pl.ds(start, tb), h, :], kbuf.at[slot, h],
                                             sem.at[slot, 0]))
            out.append(pltpu.make_async_copy(cv_ref.at[0, b, pl.ds(start, tb), h, :], vbuf.at[slot, h],
                                             sem.at[slot, 1]))
        return out

    for c in block_copies(0, 0):
        c.start()

    q = q_ref[...]
    qh = [q[:, h * hd:(h + 1) * hd].astype(BF16) for h in range(n_heads)]

    pad = jnp.zeros((SB_NEW_PAD - t, d), F32)
    kn = jnp.concatenate([kn_ref[...], pad], axis=0).astype(BF16)
    vn = jnp.concatenate([vn_ref[...], pad], axis=0).astype(BF16)
    z = jnp.concatenate(
        [lax.dot_general(qh[h], kn[:, h * hd:(h + 1) * hd], NT_DIMS, preferred_element_type=F32)
         for h in range(n_heads)], axis=0) * scale
    rr = lax.broadcasted_iota(jnp.int32, (rows, SB_NEW_PAD), 0) % t
    cc = lax.broadcasted_iota(jnp.int32, (rows, SB_NEW_PAD), 1)
    a, carry = _sb_weights(z, cc < rr, jnp.zeros((rows, 1), F32), u[:SB_NEW_PAD, :SB_NEW_PAD])
    a = a.astype(BF16)
    acc = jnp.concatenate(
        [jnp.dot(a[h * t:(h + 1) * t], vn[:, h * hd:(h + 1) * hd], preferred_element_type=F32)
         for h in range(n_heads)], axis=1)

    def more(state):
        j, cmax, _, _ = state
        return jnp.logical_and(j < n_blocks, cmax > SB_STOP_BELOW)

    def cache_block(state):
        j, _, carry, acc = state
        slot = j % 2
        for c in block_copies(j, slot):
            c.wait()

        @pl.when(j + 1 < n_blocks)
        def _():
            for c in block_copies(j + 1, 1 - slot):
                c.start()

        z = jnp.concatenate(
            [lax.dot_general(qh[h], kbuf[slot, h].astype(BF16), NT_DIMS, preferred_element_type=F32)
             for h in range(n_heads)], axis=0) * scale
        a, carry = _sb_weights(z, None, carry, u)
        a = a.astype(BF16)
        out = jnp.concatenate(
            [jnp.dot(a[h * t:(h + 1) * t], vbuf[slot, h].astype(BF16), preferred_element_type=F32)
             for h in range(n_heads)], axis=1)
        return j + 1, jnp.max(carry), carry, acc + out

    j, _, _, acc = lax.while_loop(more, cache_block, (jnp.int32(0), jnp.max(carry), carry, acc))
    o_ref[...] = acc.astype(o_ref.dtype)

    @pl.when(j < n_blocks)
    def _():
        for c in block_copies(j, j % 2):
            c.wait()


def _sb_sample(qkv, cache_k, cache_v, u, row0, batch, t, n_heads):
    hd = HEAD_DIM_B
    d = n_heads * hd
    past = cache_k.shape[2]
    assert past % SB_BLOCK == 0 and t <= SB_NEW_PAD
    blk0 = row0 // t
    return pl.pallas_call(
        functools.partial(_sb_sample_kernel, t=t, n_heads=n_heads, n_blocks=past // SB_BLOCK),
        grid=(batch,),
        in_specs=[pl.BlockSpec((t, d), lambda b: (blk0 + b, 0)),
                  pl.BlockSpec((t, d), lambda b: (blk0 + b, 1)),
                  pl.BlockSpec((t, d), lambda b: (blk0 + b, 2)),
                  pl.BlockSpec((SB_BLOCK, SB_BLOCK), lambda b: (0, 0)),
                  pl.BlockSpec(memory_space=pl.ANY),
                  pl.BlockSpec(memory_space=pl.ANY)],
        out_specs=pl.BlockSpec((t, d), lambda b: (b, 0)),
        out_shape=jax.ShapeDtypeStruct((batch * t, d), BF16),
        scratch_shapes=[pltpu.VMEM((2, n_heads, SB_BLOCK, hd), F32),
                        pltpu.VMEM((2, n_heads, SB_BLOCK, hd), F32),
                        pltpu.SemaphoreType.DMA((2, 2))],
        compiler_params=_cparams(("arbitrary",)),
        name="sb_sample",
    )(qkv, qkv, qkv, u, cache_k, cache_v)


def _pack_bf16_pair(a, b):
    ab = lax.bitcast_convert_type(a.astype(BF16).astype(F32), jnp.uint32)
    bb = lax.bitcast_convert_type(b.astype(BF16).astype(F32), jnp.uint32)
    return (ab >> 16) | (bb & jnp.uint32(0xFFFF0000))


def _unpack_bf16_pair(p):
    a = lax.bitcast_convert_type(p << 16, F32)
    b = lax.bitcast_convert_type(p & jnp.uint32(0xFFFF0000), F32)
    return a.astype(BF16), b.astype(BF16)


def _oproj_router_kernel(op_ref, os_ref, x_ref, wo_ref, g_ref, wr_ref, br_ref, x1_ref, tp_ref, route_ref,
                         *, prompt_tiles):
    o = jnp.where(pl.program_id(0) < prompt_tiles, op_ref[...], os_ref[...])
    x1 = x_ref[...] + jnp.dot(o, wo_ref[...], preferred_element_type=F32)
    x1_ref[...] = x1
    ms = jnp.mean(x1 * x1, axis=-1, keepdims=True)
    t = x1 * lax.rsqrt(ms + RMS_EPS) * g_ref[...]
    half = t.shape[1] // 2
    tp_ref[...] = _pack_bf16_pair(t[:, :half], t[:, half:])
    hi = t.astype(BF16)
    lo = (t - hi.astype(F32)).astype(BF16)
    r_hi = jnp.dot(hi, wr_ref[...], preferred_element_type=F32)
    r_lo = jnp.dot(lo, wr_ref[...], preferred_element_type=F32)
    logits = r_hi + pltpu.roll(r_hi, LANES - ROUTER_LO_LANE, 1) + r_lo + br_ref[...]

    lane = lax.broadcasted_iota(jnp.int32, logits.shape, 1)
    lane_f = lane.astype(F32)
    big = float(LANES)
    is_group = lane < N_GROUPS
    gl = jnp.where(is_group, logits, -jnp.inf)
    gmax = jnp.max(gl, axis=-1, keepdims=True)
    gidx = jnp.min(jnp.where(gl == gmax, lane_f, big), axis=-1, keepdims=True)
    gsum = jnp.sum(jnp.where(is_group, jnp.exp(logits - gmax), 0.0), axis=-1, keepdims=True)
    g_w = 1.0 / gsum
    expert = lane - GATE_LANE0
    in_group = (expert >= 0) & (expert < N_EXPERTS) & ((expert // EXPERTS_PER_GROUP).astype(F32) == gidx)
    el = jnp.where(in_group, logits, -jnp.inf)
    v1 = jnp.max(el, axis=-1, keepdims=True)
    i1 = jnp.min(jnp.where(el == v1, lane_f, big), axis=-1, keepdims=True)
    el2 = jnp.where(lane_f == i1, -jnp.inf, el)
    v2 = jnp.max(el2, axis=-1, keepdims=True)
    i2 = jnp.min(jnp.where(el2 == v2, lane_f, big), axis=-1, keepdims=True)
    e21 = jnp.exp(v2 - v1)
    w1 = g_w / (1.0 + e21)
    w2 = g_w * e21 / (1.0 + e21)
    route_ref[...] = jnp.where(lane == 0, i1 - GATE_LANE0,
                               jnp.where(lane == 1, i2 - GATE_LANE0,
                                         jnp.where(lane == 2, w1, jnp.where(lane == 3, w2, 0.0))))


def _split_tiles(n_p, n_s, pref):
    tm = _pick(math.gcd(n_p, n_s), pref, 16)
    return tm, n_p // tm


def _oproj_router(o_p, o_s, x, wo, g, wr, br):
    n, d = x.shape
    tm, prompt_tiles = _split_tiles(o_p.shape[0], o_s.shape[0], 256)
    row = lambda i: (i, 0)
    const = lambda i: (0, 0)
    return pl.pallas_call(
        functools.partial(_oproj_router_kernel, prompt_tiles=prompt_tiles),
        grid=(n // tm,),
        in_specs=[pl.BlockSpec((tm, d), lambda i: (jnp.minimum(i, prompt_tiles - 1), 0)),
                  pl.BlockSpec((tm, d), lambda i: (jnp.maximum(i - prompt_tiles, 0), 0)),
                  pl.BlockSpec((tm, d), row),
                  pl.BlockSpec((d, d), const), pl.BlockSpec((1, d), const),
                  pl.BlockSpec((d, LANES), const), pl.BlockSpec((1, LANES), const)],
        out_specs=[pl.BlockSpec((tm, d), row), pl.BlockSpec((tm, d // 2), row), pl.BlockSpec((tm, LANES), row)],
        out_shape=[jax.ShapeDtypeStruct((n, d), F32), jax.ShapeDtypeStruct((n, d // 2), jnp.uint32),
                   jax.ShapeDtypeStruct((n, LANES), F32)],
        compiler_params=_cparams(("arbitrary",)),
        name="oproj_router",
    )(o_p, o_s, x, wo, g.reshape(1, d), wr, br)


MOE_TILE = 256
TOP_K = 2


def _moe_plan(route, n_tiles):
    n = route.shape[0]
    e_flat = route[:, :TOP_K].astype(jnp.int32).reshape(n * TOP_K)
    onehot = (e_flat[:, None] == jnp.arange(N_EXPERTS, dtype=jnp.int32)[None, :]).astype(jnp.int32)
    csum = jnp.cumsum(onehot, axis=0)
    rank = jnp.sum(csum * onehot, axis=1) - 1
    counts = csum[-1]
    tiles = (counts + MOE_TILE - 1) // MOE_TILE
    tile_end = jnp.cumsum(tiles)
    row_start = (tile_end - tiles) * MOE_TILE
    pos = jnp.sum(onehot * row_start[None, :], axis=1) + rank
    n_used = tile_end[-1]
    tile_id = jnp.minimum(jnp.arange(n_tiles, dtype=jnp.int32), n_used - 1)
    tile_expert = jnp.sum((tile_end[None, :] <= tile_id[:, None]).astype(jnp.int32), axis=1)
    return pos.astype(jnp.int32), tile_expert.astype(jnp.int32), n_used.reshape(1).astype(jnp.int32)


def _dispatch_kernel(pos_ref, tp_ref, xs_in_ref, xs_ref, sem, *, tc):
    del xs_in_ref
    base = pl.program_id(0) * tc

    def issue(j, carry):
        for s in range(TOP_K):
            p = pos_ref[TOP_K * (base + j) + s]
            pltpu.make_async_copy(tp_ref.at[pl.ds(j, 1)], xs_ref.at[pl.ds(p, 1)], sem).start()
        return carry

    lax.fori_loop(0, tc, issue, 0)
    for _ in range(TOP_K):
        pltpu.make_async_copy(tp_ref, xs_ref.at[pl.ds(0, tc)], sem).wait()


def _dispatch(pos, tp, n_rows_pad):
    n, dw = tp.shape
    tc = _pick(n, 1536)
    assert tc <= n_rows_pad
    grid_spec = pltpu.PrefetchScalarGridSpec(
        num_scalar_prefetch=1, grid=(n // tc,),
        in_specs=[pl.BlockSpec((tc, dw), lambda i, pos: (i, 0)), pl.BlockSpec(memory_space=pl.ANY)],
        out_specs=pl.BlockSpec(memory_space=pl.ANY),
        scratch_shapes=[pltpu.SemaphoreType.DMA(())])
    return pl.pallas_call(
        functools.partial(_dispatch_kernel, tc=tc),
        grid_spec=grid_spec,
        out_shape=jax.ShapeDtypeStruct((n_rows_pad, dw), jnp.uint32),
        input_output_aliases={2: 0},
        compiler_params=_cparams(("arbitrary",)),
        name="moe_dispatch",
    )(pos, tp, jnp.zeros((n_rows_pad, dw), jnp.uint32))


def _experts_kernel(te_ref, nu_ref, xs_ref, wg_ref, wu_ref, wd_ref, ys_ref, wg_s, wu_s, wd_s):
    i = pl.program_id(0)
    used = i < nu_ref[0]
    new_expert = jnp.logical_or(i == 0, te_ref[i] != te_ref[jnp.maximum(i - 1, 0)])

    @pl.when(jnp.logical_and(used, new_expert))
    def _():
        wg_s[...] = wg_ref[0, 0].astype(BF16)
        wu_s[...] = wu_ref[0, 0].astype(BF16)
        wd_s[...] = wd_ref[0, 0].astype(BF16)

    @pl.when(jnp.logical_not(used))
    def _():
        ys_ref[...] = jnp.zeros_like(ys_ref)

    @pl.when(used)
    def _():
        a, b = _unpack_bf16_pair(xs_ref[...])
        x = jnp.concatenate([a, b], axis=1)
        hg = jnp.dot(x, wg_s[...], preferred_element_type=F32)
        hu = jnp.dot(x, wu_s[...], preferred_element_type=F32)
        hid = (hg * jax.nn.sigmoid(hg) * hu).astype(BF16)
        ys_ref[...] = jnp.dot(hid, wd_s[...], preferred_element_type=F32)


def _experts(tile_expert, n_used, xs, wg, wu, wd, layer):
    rows, dw = xs.shape
    _, _, d, de = wg.shape
    grid_spec = pltpu.PrefetchScalarGridSpec(
        num_scalar_prefetch=2, grid=(rows // MOE_TILE,),
        in_specs=[pl.BlockSpec((MOE_TILE, dw), lambda i, te, nu: (i, 0)),
                  pl.BlockSpec((1, 1, d, de), lambda i, te, nu: (layer, te[i], 0, 0)),
                  pl.BlockSpec((1, 1, d, de), lambda i, te, nu: (layer, te[i], 0, 0)),
                  pl.BlockSpec((1, 1, de, d), lambda i, te, nu: (layer, te[i], 0, 0))],
        out_specs=pl.BlockSpec((MOE_TILE, d), lambda i, te, nu: (i, 0)),
        scratch_shapes=[pltpu.VMEM((d, de), BF16), pltpu.VMEM((d, de), BF16), pltpu.VMEM((de, d), BF16)])
    return pl.pallas_call(
        _experts_kernel,
        grid_spec=grid_spec,
        out_shape=jax.ShapeDtypeStruct((rows, d), F32),
        compiler_params=_cparams(("arbitrary",)),
        name="moe_experts",
    )(tile_expert, n_used, xs, wg, wu, wd)


def _combine_kernel(pos_ref, ys_ref, x1_ref, route_ref, *rest, tc, prompt_tiles):
    if prompt_tiles is None:
        o_ref, buf_ref, sem = rest
    else:
        g_ref, yp_ref, ysm_ref, buf_ref, sem = rest
    i = pl.program_id(0)
    n_steps = pl.num_programs(0)

    def issue(step, slot):
        def body(j, carry):
            tok = step * tc + j
            for s in range(TOP_K):
                p = pos_ref[TOP_K * tok + s]
                pltpu.make_async_copy(ys_ref.at[pl.ds(p, 1)], buf_ref.at[slot, s, pl.ds(j, 1)],
                                      sem.at[slot]).start()
            return carry
        lax.fori_loop(0, tc, body, 0)

    @pl.when(i == 0)
    def _():
        issue(0, 0)

    @pl.when(i + 1 < n_steps)
    def _():
        issue(i + 1, (i + 1) % 2)

    slot = i % 2
    for s in range(TOP_K):
        pltpu.make_async_copy(ys_ref.at[pl.ds(0, tc)], buf_ref.at[slot, s], sem.at[slot]).wait()
    route = route_ref[...]
    w0 = route[:, TOP_K:TOP_K + 1]
    w1 = route[:, TOP_K + 1:TOP_K + 2]
    x = x1_ref[...] + buf_ref[slot, 0] * w0 + buf_ref[slot, 1] * w1
    if prompt_tiles is None:
        o_ref[...] = x
    else:
        ms = jnp.mean(x * x, axis=-1, keepdims=True)
        y = x * lax.rsqrt(ms + RMS_EPS) * g_ref[...]

        @pl.when(i < prompt_tiles)
        def _():
            yp_ref[...] = y

        @pl.when(i >= prompt_tiles)
        def _():
            ysm_ref[...] = y


def _combine(pos, ys, x1, route, n_p, final_gain=None):
    n, d = x1.shape
    tc, prompt_tiles = _split_tiles(n_p, n - n_p, 256)
    row = lambda i, pos: (i, 0)
    in_specs = [pl.BlockSpec(memory_space=pl.ANY), pl.BlockSpec((tc, d), row), pl.BlockSpec((tc, LANES), row)]
    args = [pos, ys, x1, route]
    if final_gain is None:
        out_specs = pl.BlockSpec((tc, d), row)
        out_shape = jax.ShapeDtypeStruct((n, d), F32)
    else:
        in_specs.append(pl.BlockSpec((1, d), lambda i, pos: (0, 0)))
        args.append(final_gain.reshape(1, d))
        out_specs = [pl.BlockSpec((tc, d), lambda i, pos: (jnp.minimum(i, prompt_tiles - 1), 0)),
                     pl.BlockSpec((tc, d), lambda i, pos: (jnp.maximum(i - prompt_tiles, 0), 0))]
        out_shape = [jax.ShapeDtypeStruct((n_p, d), F32), jax.ShapeDtypeStruct((n - n_p, d), F32)]
    grid_spec = pltpu.PrefetchScalarGridSpec(
        num_scalar_prefetch=1, grid=(n // tc,), in_specs=in_specs, out_specs=out_specs,
        scratch_shapes=[pltpu.VMEM((2, TOP_K, tc, d), F32), pltpu.SemaphoreType.DMA((2,))])
    return pl.pallas_call(
        functools.partial(_combine_kernel, tc=tc, prompt_tiles=None if final_gain is None else prompt_tiles),
        grid_spec=grid_spec,
        out_shape=out_shape,
        compiler_params=_cparams(("arbitrary",)),
        name="moe_combine",
    )(*args)


def _moe(tp, route, x1, wg, wu, wd, layer, n_p, final_gain=None):
    n = x1.shape[0]
    n_tiles = -(-n * TOP_K // MOE_TILE) + N_EXPERTS
    pos, tile_expert, n_used = _moe_plan(route, n_tiles)
    xs = _dispatch(pos, tp, n_tiles * MOE_TILE)
    ys = _experts(tile_expert, n_used, xs, wg, wu, wd, layer)
    return _combine(pos, ys, x1, route, n_p, final_gain)


def _rope_tables(pos):
    half = HEAD_DIM_A // 2
    inv = ROPE_THETA ** (-jnp.arange(half, dtype=F32) / half)
    ang = pos.astype(F32)[:, None] * inv[None, :]
    cos = jnp.cos(ang)
    sin = jnp.sin(ang)
    reps = LANES // HEAD_DIM_A
    return jnp.tile(jnp.concatenate([cos, cos], axis=1), (1, reps)), \
        jnp.tile(jnp.concatenate([-sin, sin], axis=1), (1, reps))


def _router_weights(w_group, b_group, w_router, b_router):
    d = w_group.shape[0]
    w = jnp.concatenate([w_group, w_router], axis=1).astype(F32)
    n_log = w.shape[1]
    hi = w.astype(BF16)
    lo = (w - hi.astype(F32)).astype(BF16)
    wr = jnp.zeros((d, LANES), BF16)
    wr = wr.at[:, :n_log].set(hi).at[:, ROUTER_LO_LANE:ROUTER_LO_LANE + n_log].set(lo)
    br = jnp.zeros((1, LANES), F32).at[0, :n_log].set(jnp.concatenate([b_group, b_router]).astype(F32))
    return wr, br


def kernel(x_prompt, x_sample, cache_win_k, cache_win_v, cache_sb_k, cache_sb_v, norm_mix, norm_ffn, norm_final,
           a_w_qkv, a_b_qkv, a_sinks, a_w_o, b_w_qkv, b_w_o, moe_w_group, moe_b_group, moe_w_router,
           moe_b_router, moe_w_gate, moe_w_up, moe_w_down):
    bp, sp, d = x_prompt.shape
    bs, ts, _ = x_sample.shape
    n_p = bp * sp
    n_s = bs * ts
    past = cache_sb_k.shape[2]
    n_kv = cache_win_k.shape[3]
    n_heads_b = cache_sb_k.shape[3]
    nq_a = n_kv * GROUP_A * HEAD_DIM_A
    nk_a = n_kv * HEAD_DIM_A
    assert sp % CHUNK == 0 and past % CHUNK == 0 and ts <= CHUNK and n_p % ts == 0
    assert cache_win_k.shape[2] == WINDOW and d == n_heads_b * HEAD_DIM_B == nq_a

    x = jnp.concatenate([x_prompt.reshape(n_p, d), x_sample.reshape(n_s, d)], axis=0)
    pos = jnp.concatenate([jnp.tile(jnp.arange(sp, dtype=jnp.int32), bp),
                           jnp.tile(past + jnp.arange(ts, dtype=jnp.int32), bs)])
    cos, sin = _rope_tables(pos)
    u = (lax.broadcasted_iota(jnp.int32, (SB_BLOCK, SB_BLOCK), 0)
         > lax.broadcasted_iota(jnp.int32, (SB_BLOCK, SB_BLOCK), 1)).astype(BF16)

    outs = {}
    for i in range(2):
        if i == 0:
            tn = _pick(a_w_qkv.shape[2], 1280)
            qkv = _norm_proj(x, norm_mix[i], a_w_qkv[0].astype(BF16), a_b_qkv[0], cos, sin, nq_a + nk_a, tn)
            sink_rows = jnp.repeat(a_sinks[0].reshape(n_kv, GROUP_A), CHUNK, axis=1)[..., None]
            o_p = _win_prompt(qkv, sink_rows, bp, sp, n_kv)
            sink_rows_s = jnp.repeat(a_sinks[0].reshape(n_kv, GROUP_A), ts, axis=1)[..., None]
            o_s = _win_sample(qkv, cache_win_k[0].reshape(bs, WINDOW, nk_a), cache_win_v[0].reshape(bs, WINDOW, nk_a),
                              sink_rows_s, n_p, bs, ts, n_kv)
            k_all = qkv[:, nq_a:nq_a + nk_a]
            v_all = qkv[:, nq_a + nk_a:]
            outs["wkp"] = k_all[:n_p].reshape(bp, sp, n_kv, HEAD_DIM_A)[:, sp - WINDOW:][None]
            outs["wvp"] = v_all[:n_p].reshape(bp, sp, n_kv, HEAD_DIM_A)[:, sp - WINDOW:][None]
            outs["wks"] = jnp.concatenate(
                [cache_win_k[0], k_all[n_p:].reshape(bs, ts, n_kv, HEAD_DIM_A)], axis=1)[:, -WINDOW:][None]
            outs["wvs"] = jnp.concatenate(
                [cache_win_v[0], v_all[n_p:].reshape(bs, ts, n_kv, HEAD_DIM_A)], axis=1)[:, -WINDOW:][None]
            w_o = a_w_o[0]
        else:
            zeros_b = jnp.zeros((b_w_qkv.shape[2],), F32)
            qkv = _norm_proj(x, norm_mix[i], b_w_qkv[0].astype(BF16), zeros_b, cos, sin, 0, d)
            o_p = _sb_prompt(qkv, u, bp, sp, n_heads_b)
            o_s = _sb_sample(qkv, cache_sb_k, cache_sb_v, u, n_p, bs, ts, n_heads_b)
            k_all = qkv[:, d:2 * d]
            v_all = qkv[:, 2 * d:]
            outs["skp"] = k_all[:n_p].reshape(1, bp, sp, n_heads_b, HEAD_DIM_B)
            outs["svp"] = v_all[:n_p].reshape(1, bp, sp, n_heads_b, HEAD_DIM_B)
            outs["sks"] = k_all[n_p:].reshape(1, bs, ts, n_heads_b, HEAD_DIM_B)
            outs["svs"] = v_all[n_p:].reshape(1, bs, ts, n_heads_b, HEAD_DIM_B)
            w_o = b_w_o[0]
        wr, br = _router_weights(moe_w_group[i], moe_b_group[i], moe_w_router[i], moe_b_router[i])
        x1, tp, route = _oproj_router(o_p, o_s, x, w_o.astype(BF16), norm_ffn[i], wr, br)
        x = _moe(tp, route, x1, moe_w_gate, moe_w_up, moe_w_down, i, n_p,
                 final_gain=norm_final if i == 1 else None)
    y_p, y_s = x
    return (y_p.reshape(bp, sp, d), y_s.reshape(bs, ts, d),
            outs["wkp"], outs["wvp"], outs["wks"], outs["wvs"],
            outs["skp"], outs["svp"], outs["sks"], outs["svs"])
```

```python
import functools
import math

import jax
import jax.numpy as jnp
from jax import lax
from jax.experimental import pallas as pl
from jax.experimental.pallas import tpu as pltpu

F32 = jnp.float32
BF16 = jnp.bfloat16

CHUNK = 64
WINDOW = 128
HEAD_DIM_A = 64
GROUP_A = 8
HEAD_DIM_B = 128
N_GROUPS = 4
EXPERTS_PER_GROUP = 8
N_EXPERTS = N_GROUPS * EXPERTS_PER_GROUP
ROPE_THETA = 10000.0
RMS_EPS = 1e-6
NEG_INF = -1e30

LANES = 128
ROUTER_LO_LANE = 64
GATE_LANE0 = N_GROUPS
VMEM_LIMIT = 56 * 1024 * 1024

NT_DIMS = (((1,), (1,)), ((), ()))


def _cparams(sem):
    return pltpu.CompilerParams(dimension_semantics=sem, vmem_limit_bytes=VMEM_LIMIT)


def _pick(n, pref, align=8):
    for t in range(min(n, pref), 0, -1):
        if n % t == 0 and t % align == 0:
            return t
    raise ValueError((n, pref, align))


def _norm_proj_kernel(x_ref, g_ref, w_ref, b_ref, cos_ref, sin_ref, o_ref, h_ref, *, rope_cols, tn):
    j = pl.program_id(1)

    @pl.when(j == 0)
    def _():
        x = x_ref[...]
        ms = jnp.mean(x * x, axis=-1, keepdims=True)
        h_ref[...] = (x * lax.rsqrt(ms + RMS_EPS) * g_ref[...]).astype(BF16)

    y = jnp.dot(h_ref[...], w_ref[...], preferred_element_type=F32) + b_ref[...]
    if rope_cols:
        reps = tn // LANES
        cos = jnp.concatenate([cos_ref[...]] * reps, axis=1)
        sin = jnp.concatenate([sin_ref[...]] * reps, axis=1)
        lane = lax.broadcasted_iota(jnp.int32, y.shape, 1)
        half = HEAD_DIM_A // 2
        first = (lane % HEAD_DIM_A) < half
        swapped = jnp.where(first, pltpu.roll(y, tn - half, 1), pltpu.roll(y, half, 1))
        roped = y * cos + swapped * sin
        y = jnp.where(lane + j * tn < rope_cols, roped, y)
    o_ref[...] = y


def _norm_proj(x, g, w, b, cos, sin, rope_cols, tn):
    n, d = x.shape
    nout = w.shape[1]
    tm = _pick(n, 512)
    kern = functools.partial(_norm_proj_kernel, rope_cols=rope_cols, tn=tn)
    return pl.pallas_call(
        kern,
        grid=(n // tm, nout // tn),
        in_specs=[
            pl.BlockSpec((tm, d), lambda i, j: (i, 0)),
            pl.BlockSpec((1, d), lambda i, j: (0, 0)),
            pl.BlockSpec((d, tn), lambda i, j: (0, j)),
            pl.BlockSpec((1, tn), lambda i, j: (0, j)),
            pl.BlockSpec((tm, LANES), lambda i, j: (i, 0)),
            pl.BlockSpec((tm, LANES), lambda i, j: (i, 0)),
        ],
        out_specs=pl.BlockSpec((tm, tn), lambda i, j: (i, j)),
        out_shape=jax.ShapeDtypeStruct((n, nout), F32),
        scratch_shapes=[pltpu.VMEM((tm, d), BF16)],
        compiler_params=_cparams(("arbitrary", "arbitrary")),
        name="norm_proj",
    )(x, g.reshape(1, d), w, b.reshape(1, nout), cos, sin)


WIN_KEYS_PAD = 256
HALF = LANES // 2


def _sink_attention(q, k, v, sinks_ref, lo, n_keys, n_kv):
    t = q.shape[0]
    sp = WIN_KEYS_PAD
    assert HEAD_DIM_A == HALF and n_kv % 2 == 0 and n_keys < sp
    qb = (q * (HEAD_DIM_A ** -0.5)).astype(BF16)
    low_half = lax.broadcasted_iota(jnp.int32, (sp, LANES), 1) < HALF
    halves = (low_half, jnp.logical_not(low_half))
    ones = tuple(jnp.where(h, 1.0, 0.0).astype(BF16) for h in halves)
    col = lax.broadcasted_iota(jnp.int32, (1, sp), 1)
    visible = jnp.logical_and(col >= lo, col < n_keys)
    is_sink = col == n_keys

    kmat, vmat = {}, {}
    for slab in range(n_kv // 2):
        ks = k[:, slab * LANES:(slab + 1) * LANES]
        vs = v[:, slab * LANES:(slab + 1) * LANES]
        moved = (pltpu.roll(ks, HALF, 1), pltpu.roll(vs, HALF, 1))
        for gp in range(2):
            for par in range(2):
                src_k, src_v = (ks, vs) if par == gp else moved
                kmat[2 * slab + gp, par] = jnp.where(halves[par], src_k, 0.0).astype(BF16)
                vmat[2 * slab + gp, par] = jnp.concatenate(
                    [jnp.where(halves[par], src_v, 0.0).astype(BF16), ones[par]], axis=1)

    ppg = GROUP_A // 2
    outs = []
    for g in range(n_kv):
        qg = jnp.concatenate([qb[:, (g * ppg + i) * LANES:(g * ppg + i + 1) * LANES] for i in range(ppg)], axis=0)
        acc = None
        for par in range(2):
            s = lax.dot_general(qg, kmat[g, par], NT_DIMS, preferred_element_type=F32)
            s = jnp.where(visible, s, NEG_INF)
            s = jnp.concatenate(
                [jnp.where(is_sink, sinks_ref[2 * (g * ppg + i) + par], s[i * t:(i + 1) * t]) for i in range(ppg)],
                axis=0)
            p = jnp.exp(s - jnp.max(s, axis=-1, keepdims=True)).astype(BF16)
            o = jnp.dot(p, vmat[g, par], preferred_element_type=F32)
            acc = o if acc is None else acc + o
        o_norm = acc[:, :LANES] / acc[:, LANES:]
        outs.extend(o_norm[i * t:(i + 1) * t] for i in range(ppg))
    return jnp.concatenate(outs, axis=1)


def _win_prompt_kernel(sinks_ref, q_ref, k0_ref, k1_ref, k2_ref, v0_ref, v1_ref, v2_ref, o_ref, *, n_kv):
    c = pl.program_id(1)
    pad = jnp.zeros((WIN_KEYS_PAD - 3 * CHUNK, k0_ref.shape[1]), F32)
    k = jnp.concatenate([k0_ref[...], k1_ref[...], k2_ref[...], pad], axis=0)
    v = jnp.concatenate([v0_ref[...], v1_ref[...], v2_ref[...], pad], axis=0)
    lo = jnp.maximum(2 - c, 0) * CHUNK
    o_ref[...] = _sink_attention(q_ref[...], k, v, sinks_ref, lo, 3 * CHUNK, n_kv).astype(o_ref.dtype)


def _win_prompt(qkv, sinks, batch, seq, n_kv):
    nq = n_kv * GROUP_A * HEAD_DIM_A
    nk = n_kv * HEAD_DIM_A
    nc = seq // CHUNK
    kcol = nq // nk
    q_spec = pl.BlockSpec((CHUNK, nq), lambda b, c: (b * nc + c, 0))

    def kv_spec(back, col):
        return pl.BlockSpec((CHUNK, nk), lambda b, c: (b * nc + jnp.maximum(c - back, 0), col))

    return pl.pallas_call(
        functools.partial(_win_prompt_kernel, n_kv=n_kv),
        grid=(batch, nc),
        in_specs=[pl.BlockSpec(memory_space=pltpu.SMEM),
                  q_spec, kv_spec(2, kcol), kv_spec(1, kcol), kv_spec(0, kcol),
                  kv_spec(2, kcol + 1), kv_spec(1, kcol + 1), kv_spec(0, kcol + 1)],
        out_specs=pl.BlockSpec((CHUNK, nq), lambda b, c: (b * nc + c, 0)),
        out_shape=jax.ShapeDtypeStruct((batch * seq, nq), BF16),
        compiler_params=_cparams(("arbitrary", "arbitrary")),
        name="win_prompt",
    )(sinks, qkv, qkv, qkv, qkv, qkv, qkv, qkv)


def _win_sample_kernel(sinks_ref, q_ref, kn_ref, vn_ref, ck_ref, cv_ref, o_ref, *, n_kv):
    t = kn_ref.shape[0]
    pad = jnp.zeros((WIN_KEYS_PAD - WINDOW - t, kn_ref.shape[1]), F32)
    k = jnp.concatenate([ck_ref[0], kn_ref[...], pad], axis=0)
    v = jnp.concatenate([cv_ref[0], vn_ref[...], pad], axis=0)
    o_ref[...] = _sink_attention(q_ref[...], k, v, sinks_ref, 0, WINDOW + t, n_kv).astype(o_ref.dtype)


def _win_sample(qkv, cache_k, cache_v, sinks, row0, batch, t, n_kv):
    nq = n_kv * GROUP_A * HEAD_DIM_A
    nk = n_kv * HEAD_DIM_A
    kcol = nq // nk
    blk0 = row0 // t
    return pl.pallas_call(
        functools.partial(_win_sample_kernel, n_kv=n_kv),
        grid=(batch,),
        in_specs=[pl.BlockSpec(memory_space=pltpu.SMEM),
                  pl.BlockSpec((t, nq), lambda b: (blk0 + b, 0)),
                  pl.BlockSpec((t, nk), lambda b: (blk0 + b, kcol)),
                  pl.BlockSpec((t, nk), lambda b: (blk0 + b, kcol + 1)),
                  pl.BlockSpec((1, WINDOW, nk), lambda b: (b, 0, 0)),
                  pl.BlockSpec((1, WINDOW, nk), lambda b: (b, 0, 0))],
        out_specs=pl.BlockSpec((t, nq), lambda b: (b, 0)),
        out_shape=jax.ShapeDtypeStruct((batch * t, nq), BF16),
        compiler_params=_cparams(("arbitrary",)),
        name="win_sample",
    )(sinks, qkv, qkv, qkv, cache_k, cache_v)


SB_STOP_BELOW = -110.0
SB_BLOCK = 256


def _sb_weights(z, mask, carry, u):
    sp = jnp.maximum(z, 0.0) + jnp.log(1.0 + jnp.exp(-jnp.abs(z)))
    ls = -sp if mask is None else jnp.where(mask, -sp, 0.0)
    hi = ls.astype(BF16)
    lo = (ls - hi.astype(F32)).astype(BF16)
    after = (jnp.dot(hi, u, preferred_element_type=F32) + jnp.dot(lo, u, preferred_element_type=F32)) + carry
    a = jnp.exp(z - sp + after)
    if mask is not None:
        a = jnp.where(mask, a, 0.0)
    return a, carry + jnp.sum(ls, axis=-1, keepdims=True)


def _sb_prompt_kernel(q_ref, k_ref, v_ref, u_ref, o_ref, *, seq):
    tb = SB_BLOCK
    scale = HEAD_DIM_B ** -0.5
    u = u_ref[...]
    row = lax.broadcasted_iota(jnp.int32, (tb, tb), 0)
    col = lax.broadcasted_iota(jnp.int32, (tb, tb), 1)
    strictly_earlier = col < row
    no_carry = jnp.zeros((tb, 1), F32)

    def queries(start):
        return (q_ref[pl.ds(start, tb), :] * scale).astype(BF16)

    def scores(q, start, size):
        kb = k_ref[pl.ds(start, size), :].astype(BF16)
        return lax.dot_general(q, kb, NT_DIMS, preferred_element_type=F32)

    def weighted_values(a, start, size):
        vb = v_ref[pl.ds(start, size), :].astype(BF16)
        return jnp.dot(a.astype(BF16), vb, preferred_element_type=F32)

    a, _ = _sb_weights(scores(queries(0), 0, tb), strictly_earlier, no_carry, u)
    o_ref[pl.ds(0, tb), :] = weighted_values(a, 0, tb).astype(o_ref.dtype)

    def q_block(qi, _):
        q0 = pl.multiple_of(qi * tb, tb)
        p0 = pl.multiple_of(q0 - tb, tb)
        q = queries(q0)
        z = scores(q, p0, 2 * tb)
        a_own, carry = _sb_weights(z[:, tb:], strictly_earlier, no_carry, u)
        a_prev, carry = _sb_weights(z[:, :tb], None, carry, u)
        acc = weighted_values(jnp.concatenate([a_prev, a_own], axis=1), p0, 2 * tb)

        def more(state):
            kj, cmax, _, _ = state
            return jnp.logical_and(kj >= 0, cmax > SB_STOP_BELOW)

        def earlier_block(state):
            kj, _, carry, acc = state
            k0 = pl.multiple_of(kj * tb, tb)
            a, carry = _sb_weights(scores(q, k0, tb), None, carry, u)
            return kj - 1, jnp.max(carry), carry, acc + weighted_values(a, k0, tb)

        _, _, _, acc = lax.while_loop(more, earlier_block, (qi - 2, jnp.max(carry), carry, acc))
        o_ref[pl.ds(q0, tb), :] = acc.astype(o_ref.dtype)
        return 0

    lax.fori_loop(1, seq // tb, q_block, 0)


def _sb_prompt(qkv, u, batch, seq, n_heads):
    hd = HEAD_DIM_B
    assert seq % SB_BLOCK == 0
    return pl.pallas_call(
        functools.partial(_sb_prompt_kernel, seq=seq),
        grid=(batch, n_heads),
        in_specs=[pl.BlockSpec((seq, hd), lambda b, h: (b, h)),
                  pl.BlockSpec((seq, hd), lambda b, h: (b, n_heads + h)),
                  pl.BlockSpec((seq, hd), lambda b, h: (b, 2 * n_heads + h)),
                  pl.BlockSpec((SB_BLOCK, SB_BLOCK), lambda b, h: (0, 0))],
        out_specs=pl.BlockSpec((seq, hd), lambda b, h: (b, h)),
        out_shape=jax.ShapeDtypeStruct((batch * seq, n_heads * hd), BF16),
        compiler_params=_cparams(("arbitrary", "arbitrary")),
        name="sb_prompt",
    )(qkv, qkv, qkv, u)


SB_NEW_PAD = SB_BLOCK


def _sb_sample_kernel(q_ref, kn_ref, vn_ref, u_ref, ck_ref, cv_ref, o_ref, kbuf, vbuf, sem,
                      *, t, n_heads, n_blocks):
    b = pl.program_id(0)
    tb = SB_BLOCK
    hd = HEAD_DIM_B
    d = n_heads * hd
    rows = n_heads * t
    scale = hd ** -0.5
    u = u_ref[...]

    def block_copies(j, slot):
        start = pl.multiple_of((n_blocks - 1 - j) * tb, tb)
        out = []
        for h in range(n_heads):
            out.append(pltpu.make_async_copy(ck_ref.at[0, b, pl.ds(start, tb), h, :], kbuf.at[slot, h],
                                             sem.at[slot, 0]))
            out.append(pltpu.make_async_copy(cv_ref.at[0, b, pl.ds(start, tb), h, :], vbuf.at[slot, h],
                                             sem.at[slot, 1]))
        return out

    for c in block_copies(0, 0):
        c.start()

    q = q_ref[...] * scale
    qh = [q[:, h * hd:(h + 1) * hd].astype(BF16) for h in range(n_heads)]

    pad = jnp.zeros((SB_NEW_PAD - t, d), F32)
    kn = jnp.concatenate([kn_ref[...], pad], axis=0).astype(BF16)
    vn = jnp.concatenate([vn_ref[...], pad], axis=0).astype(BF16)
    z = jnp.concatenate(
        [lax.dot_general(qh[h], kn[:, h * hd:(h + 1) * hd], NT_DIMS, preferred_element_type=F32)
         for h in range(n_heads)], axis=0)
    rr = lax.broadcasted_iota(jnp.int32, (rows, SB_NEW_PAD), 0) % t
    cc = lax.broadcasted_iota(jnp.int32, (rows, SB_NEW_PAD), 1)
    a, carry = _sb_weights(z, cc < rr, jnp.zeros((rows, 1), F32), u[:SB_NEW_PAD, :SB_NEW_PAD])
    a = a.astype(BF16)
    acc = jnp.concatenate(
        [jnp.dot(a[h * t:(h + 1) * t], vn[:, h * hd:(h + 1) * hd], preferred_element_type=F32)
         for h in range(n_heads)], axis=1)

    def more(state):
        j, cmax, _, _ = state
        return jnp.logical_and(j < n_blocks, cmax > SB_STOP_BELOW)

    def cache_block(state):
        j, _, carry, acc = state
        slot = j % 2
        for c in block_copies(j, slot):
            c.wait()

        @pl.when(j + 1 < n_blocks)
        def _():
            for c in block_copies(j + 1, 1 - slot):
                c.start()

        z = jnp.concatenate(
            [lax.dot_general(qh[h], kbuf[slot, h].astype(BF16), NT_DIMS, preferred_element_type=F32)
             for h in range(n_heads)], axis=0)
        a, carry = _sb_weights(z, None, carry, u)
        a = a.astype(BF16)
        out = jnp.concatenate(
            [jnp.dot(a[h * t:(h + 1) * t], vbuf[slot, h].astype(BF16), preferred_element_type=F32)
             for h in range(n_heads)], axis=1)
        return j + 1, jnp.max(carry), carry, acc + out

    j, _, _, acc = lax.while_loop(more, cache_block, (jnp.int32(0), jnp.max(carry), carry, acc))
    o_ref[...] = acc.astype(o_ref.dtype)

    @pl.when(j < n_blocks)
    def _():
        for c in block_copies(j, j % 2):
            c.wait()


def _sb_sample(qkv, cache_k, cache_v, u, row0, batch, t, n_heads):
    hd = HEAD_DIM_B
    d = n_heads * hd
    past = cache_k.shape[2]
    assert past % SB_BLOCK == 0 and t <= SB_NEW_PAD
    blk0 = row0 // t
    return pl.pallas_call(
        functools.partial(_sb_sample_kernel, t=t, n_heads=n_heads, n_blocks=past // SB_BLOCK),
        grid=(batch,),
        in_specs=[pl.BlockSpec((t, d), lambda b: (blk0 + b, 0)),
                  pl.BlockSpec((t, d), lambda b: (blk0 + b, 1)),
                  pl.BlockSpec((t, d), lambda b: (blk0 + b, 2)),
                  pl.BlockSpec((SB_BLOCK, SB_BLOCK), lambda b: (0, 0)),
                  pl.BlockSpec(memory_space=pl.ANY),
                  pl.BlockSpec(memory_space=pl.ANY)],
        out_specs=pl.BlockSpec((t, d), lambda b: (b, 0)),
        out_shape=jax.ShapeDtypeStruct((batch * t, d), BF16),
        scratch_shapes=[pltpu.VMEM((2, n_heads, SB_BLOCK, hd), F32),
                        pltpu.VMEM((2, n_heads, SB_BLOCK, hd), F32),
                        pltpu.SemaphoreType.DMA((2, 2))],
        compiler_params=_cparams(("arbitrary",)),
        name="sb_sample",
    )(qkv, qkv, qkv, u, cache_k, cache_v)


def _pack_bf16_pair(a, b):
    ab = lax.bitcast_convert_type(a.astype(BF16).astype(F32), jnp.uint32)
    bb = lax.bitcast_convert_type(b.astype(BF16).astype(F32), jnp.uint32)
    return (ab >> 16) | (bb & jnp.uint32(0xFFFF0000))


def _unpack_bf16_pair(p):
    a = lax.bitcast_convert_type(p << 16, F32)
    b = lax.bitcast_convert_type(p & jnp.uint32(0xFFFF0000), F32)
    return a.astype(BF16), b.astype(BF16)


def _oproj_router_kernel(op_ref, os_ref, x_ref, wo_ref, g_ref, wr_ref, br_ref, x1_ref, tp_ref, route_ref,
                         *, prompt_tiles):
    o = jnp.where(pl.program_id(0) < prompt_tiles, op_ref[...], os_ref[...])
    x1 = x_ref[...] + jnp.dot(o, wo_ref[...], preferred_element_type=F32)
    x1_ref[...] = x1
    ms = jnp.mean(x1 * x1, axis=-1, keepdims=True)
    t = x1 * lax.rsqrt(ms + RMS_EPS) * g_ref[...]
    half = t.shape[1] // 2
    tp_ref[...] = _pack_bf16_pair(t[:, :half], t[:, half:])
    hi = t.astype(BF16)
    lo = (t - hi.astype(F32)).astype(BF16)
    r_hi = jnp.dot(hi, wr_ref[...], preferred_element_type=F32)
    r_lo = jnp.dot(lo, wr_ref[...], preferred_element_type=F32)
    logits = r_hi + pltpu.roll(r_hi, LANES - ROUTER_LO_LANE, 1) + r_lo + br_ref[...]

    lane = lax.broadcasted_iota(jnp.int32, logits.shape, 1)
    lane_f = lane.astype(F32)
    big = float(LANES)
    is_group = lane < N_GROUPS
    gl = jnp.where(is_group, logits, -jnp.inf)
    gmax = jnp.max(gl, axis=-1, keepdims=True)
    gidx = jnp.min(jnp.where(gl == gmax, lane_f, big), axis=-1, keepdims=True)
    gsum = jnp.sum(jnp.where(is_group, jnp.exp(logits - gmax), 0.0), axis=-1, keepdims=True)
    g_w = 1.0 / gsum
    expert = lane - GATE_LANE0
    in_group = (expert >= 0) & (expert < N_EXPERTS) & ((expert // EXPERTS_PER_GROUP).astype(F32) == gidx)
    el = jnp.where(in_group, logits, -jnp.inf)
    v1 = jnp.max(el, axis=-1, keepdims=True)
    i1 = jnp.min(jnp.where(el == v1, lane_f, big), axis=-1, keepdims=True)
    el2 = jnp.where(lane_f == i1, -jnp.inf, el)
    v2 = jnp.max(el2, axis=-1, keepdims=True)
    i2 = jnp.min(jnp.where(el2 == v2, lane_f, big), axis=-1, keepdims=True)
    e21 = jnp.exp(v2 - v1)
    w1 = g_w / (1.0 + e21)
    w2 = g_w * e21 / (1.0 + e21)
    route_ref[...] = jnp.where(lane == 0, i1 - GATE_LANE0,
                               jnp.where(lane == 1, i2 - GATE_LANE0,
                                         jnp.where(lane == 2, w1, jnp.where(lane == 3, w2, 0.0))))


def _split_tiles(n_p, n_s, pref):
    tm = _pick(math.gcd(n_p, n_s), pref, 16)
    return tm, n_p // tm


def _oproj_router(o_p, o_s, x, wo, g, wr, br):
    n, d = x.shape
    tm, prompt_tiles = _split_tiles(o_p.shape[0], o_s.shape[0], 256)
    row = lambda i: (i, 0)
    const = lambda i: (0, 0)
    return pl.pallas_call(
        functools.partial(_oproj_router_kernel, prompt_tiles=prompt_tiles),
        grid=(n // tm,),
        in_specs=[pl.BlockSpec((tm, d), lambda i: (jnp.minimum(i, prompt_tiles - 1), 0)),
                  pl.BlockSpec((tm, d), lambda i: (jnp.maximum(i - prompt_tiles, 0), 0)),
                  pl.BlockSpec((tm, d), row),
                  pl.BlockSpec((d, d), const), pl.BlockSpec((1, d), const),
                  pl.BlockSpec((d, LANES), const), pl.BlockSpec((1, LANES), const)],
        out_specs=[pl.BlockSpec((tm, d), row), pl.BlockSpec((tm, d // 2), row), pl.BlockSpec((tm, LANES), row)],
        out_shape=[jax.ShapeDtypeStruct((n, d), F32), jax.ShapeDtypeStruct((n, d // 2), jnp.uint32),
                   jax.ShapeDtypeStruct((n, LANES), F32)],
        compiler_params=_cparams(("arbitrary",)),
        name="oproj_router",
    )(o_p, o_s, x, wo, g.reshape(1, d), wr, br)


MOE_TILE = 256
TOP_K = 2


def _moe_plan(route, n_tiles):
    n = route.shape[0]
    e_flat = route[:, :TOP_K].astype(jnp.int32).reshape(n * TOP_K)
    onehot = (e_flat[:, None] == jnp.arange(N_EXPERTS, dtype=jnp.int32)[None, :]).astype(jnp.int32)
    csum = jnp.cumsum(onehot, axis=0)
    rank = jnp.sum(csum * onehot, axis=1) - 1
    counts = csum[-1]
    tiles = (counts + MOE_TILE - 1) // MOE_TILE
    tile_end = jnp.cumsum(tiles)
    row_start = (tile_end - tiles) * MOE_TILE
    pos = jnp.sum(onehot * row_start[None, :], axis=1) + rank
    n_used = tile_end[-1]
    tile_id = jnp.minimum(jnp.arange(n_tiles, dtype=jnp.int32), n_used - 1)
    tile_expert = jnp.sum((tile_end[None, :] <= tile_id[:, None]).astype(jnp.int32), axis=1)
    return pos.astype(jnp.int32), tile_expert.astype(jnp.int32), n_used.reshape(1).astype(jnp.int32)


def _dispatch_kernel(pos_ref, tp_ref, xs_in_ref, xs_ref, sem, *, tc):
    del xs_in_ref
    base = pl.program_id(0) * tc

    def issue(j, carry):
        for s in range(TOP_K):
            p = pos_ref[TOP_K * (base + j) + s]
            pltpu.make_async_copy(tp_ref.at[pl.ds(j, 1)], xs_ref.at[pl.ds(p, 1)], sem).start()
        return carry

    lax.fori_loop(0, tc, issue, 0)
    for _ in range(TOP_K):
        pltpu.make_async_copy(tp_ref, xs_ref.at[pl.ds(0, tc)], sem).wait()


def _dispatch(pos, tp, n_rows_pad):
    n, dw = tp.shape
    tc = _pick(n, 1536)
    assert tc <= n_rows_pad
    grid_spec = pltpu.PrefetchScalarGridSpec(
        num_scalar_prefetch=1, grid=(n // tc,),
        in_specs=[pl.BlockSpec((tc, dw), lambda i, pos: (i, 0)), pl.BlockSpec(memory_space=pl.ANY)],
        out_specs=pl.BlockSpec(memory_space=pl.ANY),
        scratch_shapes=[pltpu.SemaphoreType.DMA(())])
    return pl.pallas_call(
        functools.partial(_dispatch_kernel, tc=tc),
        grid_spec=grid_spec,
        out_shape=jax.ShapeDtypeStruct((n_rows_pad, dw), jnp.uint32),
        input_output_aliases={2: 0},
        compiler_params=_cparams(("arbitrary",)),
        name="moe_dispatch",
    )(pos, tp, jnp.zeros((n_rows_pad, dw), jnp.uint32))


def _experts_kernel(te_ref, nu_ref, xs_ref, wg_ref, wu_ref, wd_ref, ys_ref, wg_s, wu_s, wd_s):
    i = pl.program_id(0)
    used = i < nu_ref[0]
    new_expert = jnp.logical_or(i == 0, te_ref[i] != te_ref[jnp.maximum(i - 1, 0)])

    @pl.when(jnp.logical_and(used, new_expert))
    def _():
        wg_s[...] = wg_ref[0, 0].astype(BF16)
        wu_s[...] = wu_ref[0, 0].astype(BF16)
        wd_s[...] = wd_ref[0, 0].astype(BF16)

    @pl.when(jnp.logical_not(used))
    def _():
        ys_ref[...] = jnp.zeros_like(ys_ref)

    @pl.when(used)
    def _():
        a, b = _unpack_bf16_pair(xs_ref[...])
        x = jnp.concatenate([a, b], axis=1)
        hg = jnp.dot(x, wg_s[...], preferred_element_type=F32)
        hu = jnp.dot(x, wu_s[...], preferred_element_type=F32)
        hid = (hg * jax.nn.sigmoid(hg) * hu).astype(BF16)
        ys_ref[...] = jnp.dot(hid, wd_s[...], preferred_element_type=F32)


def _experts(tile_expert, n_used, xs, wg, wu, wd, layer):
    rows, dw = xs.shape
    _, _, d, de = wg.shape
    grid_spec = pltpu.PrefetchScalarGridSpec(
        num_scalar_prefetch=2, grid=(rows // MOE_TILE,),
        in_specs=[pl.BlockSpec((MOE_TILE, dw), lambda i, te, nu: (i, 0)),
                  pl.BlockSpec((1, 1, d, de), lambda i, te, nu: (layer, te[i], 0, 0)),
                  pl.BlockSpec((1, 1, d, de), lambda i, te, nu: (layer, te[i], 0, 0)),
                  pl.BlockSpec((1, 1, de, d), lambda i, te, nu: (layer, te[i], 0, 0))],
        out_specs=pl.BlockSpec((MOE_TILE, d), lambda i, te, nu: (i, 0)),
        scratch_shapes=[pltpu.VMEM((d, de), BF16), pltpu.VMEM((d, de), BF16), pltpu.VMEM((de, d), BF16)])
    return pl.pallas_call(
        _experts_kernel,
        grid_spec=grid_spec,
        out_shape=jax.ShapeDtypeStruct((rows, d), F32),
        compiler_params=_cparams(("arbitrary",)),
        name="moe_experts",
    )(tile_expert, n_used, xs, wg, wu, wd)


def _combine_kernel(pos_ref, ys_ref, x1_ref, route_ref, *rest, tc, prompt_tiles):
    if prompt_tiles is None:
        o_ref, buf_ref, sem = rest
    else:
        g_ref, yp_ref, ysm_ref, buf_ref, sem = rest
    i = pl.program_id(0)
    n_steps = pl.num_programs(0)

    def issue(step, slot):
        def body(j, carry):
            tok = step * tc + j
            for s in range(TOP_K):
                p = pos_ref[TOP_K * tok + s]
                pltpu.make_async_copy(ys_ref.at[pl.ds(p, 1)], buf_ref.at[slot, s, pl.ds(j, 1)],
                                      sem.at[slot]).start()
            return carry
        lax.fori_loop(0, tc, body, 0)

    @pl.when(i == 0)
    def _():
        issue(0, 0)

    @pl.when(i + 1 < n_steps)
    def _():
        issue(i + 1, (i + 1) % 2)

    slot = i % 2
    for s in range(TOP_K):
        pltpu.make_async_copy(ys_ref.at[pl.ds(0, tc)], buf_ref.at[slot, s], sem.at[slot]).wait()
    route = route_ref[...]
    w0 = route[:, TOP_K:TOP_K + 1]
    w1 = route[:, TOP_K + 1:TOP_K + 2]
    x = x1_ref[...] + buf_ref[slot, 0] * w0 + buf_ref[slot, 1] * w1
    if prompt_tiles is None:
        o_ref[...] = x
    else:
        ms = jnp.mean(x * x, axis=-1, keepdims=True)
        y = x * lax.rsqrt(ms + RMS_EPS) * g_ref[...]

        @pl.when(i < prompt_tiles)
        def _():
            yp_ref[...] = y

        @pl.when(i >= prompt_tiles)
        def _():
            ysm_ref[...] = y


def _combine(pos, ys, x1, route, n_p, final_gain=None):
    n, d = x1.shape
    tc, prompt_tiles = _split_tiles(n_p, n - n_p, 256)
    row = lambda i, pos: (i, 0)
    in_specs = [pl.BlockSpec(memory_space=pl.ANY), pl.BlockSpec((tc, d), row), pl.BlockSpec((tc, LANES), row)]
    args = [pos, ys, x1, route]
    if final_gain is None:
        out_specs = pl.BlockSpec((tc, d), row)
        out_shape = jax.ShapeDtypeStruct((n, d), F32)
    else:
        in_specs.append(pl.BlockSpec((1, d), lambda i, pos: (0, 0)))
        args.append(final_gain.reshape(1, d))
        out_specs = [pl.BlockSpec((tc, d), lambda i, pos: (jnp.minimum(i, prompt_tiles - 1), 0)),
                     pl.BlockSpec((tc, d), lambda i, pos: (jnp.maximum(i - prompt_tiles, 0), 0))]
        out_shape = [jax.ShapeDtypeStruct((n_p, d), F32), jax.ShapeDtypeStruct((n - n_p, d), F32)]
    grid_spec = pltpu.PrefetchScalarGridSpec(
        num_scalar_prefetch=1, grid=(n // tc,), in_specs=in_specs, out_specs=out_specs,
        scratch_shapes=[pltpu.VMEM((2, TOP_K, tc, d), F32), pltpu.SemaphoreType.DMA((2,))])
    return pl.pallas_call(
        functools.partial(_combine_kernel, tc=tc, prompt_tiles=None if final_gain is None else prompt_tiles),
        grid_spec=grid_spec,
        out_shape=out_shape,
        compiler_params=_cparams(("arbitrary",)),
        name="moe_combine",
    )(*args)


def _moe(tp, route, x1, wg, wu, wd, layer, n_p, final_gain=None):
    n = x1.shape[0]
    n_tiles = -(-n * TOP_K // MOE_TILE) + N_EXPERTS
    pos, tile_expert, n_used = _moe_plan(route, n_tiles)
    xs = _dispatch(pos, tp, n_tiles * MOE_TILE)
    ys = _experts(tile_expert, n_used, xs, wg, wu, wd, layer)
    return _combine(pos, ys, x1, route, n_p, final_gain)


def _rope_tables(pos):
    half = HEAD_DIM_A // 2
    inv = ROPE_THETA ** (-jnp.arange(half, dtype=F32) / half)
    ang = pos.astype(F32)[:, None] * inv[None, :]
    cos = jnp.cos(ang)
    sin = jnp.sin(ang)
    reps = LANES // HEAD_DIM_A
    return jnp.tile(jnp.concatenate([cos, cos], axis=1), (1, reps)), \
        jnp.tile(jnp.concatenate([-sin, sin], axis=1), (1, reps))


def _router_weights(w_group, b_group, w_router, b_router):
    d = w_group.shape[0]
    w = jnp.concatenate([w_group, w_router], axis=1).astype(F32)
    n_log = w.shape[1]
    hi = w.astype(BF16)
    lo = (w - hi.astype(F32)).astype(BF16)
    wr = jnp.zeros((d, LANES), BF16)
    wr = wr.at[:, :n_log].set(hi).at[:, ROUTER_LO_LANE:ROUTER_LO_LANE + n_log].set(lo)
    br = jnp.zeros((1, LANES), F32).at[0, :n_log].set(jnp.concatenate([b_group, b_router]).astype(F32))
    return wr, br


def kernel(x_prompt, x_sample, cache_win_k, cache_win_v, cache_sb_k, cache_sb_v, norm_mix, norm_ffn, norm_final,
           a_w_qkv, a_b_qkv, a_sinks, a_w_o, b_w_qkv, b_w_o, moe_w_group, moe_b_group, moe_w_router,
           moe_b_router, moe_w_gate, moe_w_up, moe_w_down):
    bp, sp, d = x_prompt.shape
    bs, ts, _ = x_sample.shape
    n_p = bp * sp
    n_s = bs * ts
    past = cache_sb_k.shape[2]
    n_kv = cache_win_k.shape[3]
    n_heads_b = cache_sb_k.shape[3]
    nq_a = n_kv * GROUP_A * HEAD_DIM_A
    nk_a = n_kv * HEAD_DIM_A
    assert sp % CHUNK == 0 and past % CHUNK == 0 and ts <= CHUNK and n_p % ts == 0
    assert cache_win_k.shape[2] == WINDOW and d == n_heads_b * HEAD_DIM_B == nq_a

    x = jnp.concatenate([x_prompt.reshape(n_p, d), x_sample.reshape(n_s, d)], axis=0)
    pos = jnp.concatenate([jnp.tile(jnp.arange(sp, dtype=jnp.int32), bp),
                           jnp.tile(past + jnp.arange(ts, dtype=jnp.int32), bs)])
    cos, sin = _rope_tables(pos)
    u = (lax.broadcasted_iota(jnp.int32, (SB_BLOCK, SB_BLOCK), 0)
         > lax.broadcasted_iota(jnp.int32, (SB_BLOCK, SB_BLOCK), 1)).astype(BF16)

    outs = {}
    for i in range(2):
        if i == 0:
            tn = _pick(a_w_qkv.shape[2], 1280)
            qkv = _norm_proj(x, norm_mix[i], a_w_qkv[0].astype(BF16), a_b_qkv[0], cos, sin, nq_a + nk_a, tn)
            sinks = a_sinks[0].astype(F32)
            o_p = _win_prompt(qkv, sinks, bp, sp, n_kv)
            o_s = _win_sample(qkv, cache_win_k[0].reshape(bs, WINDOW, nk_a), cache_win_v[0].reshape(bs, WINDOW, nk_a),
                              sinks, n_p, bs, ts, n_kv)
            k_all = qkv[:, nq_a:nq_a + nk_a]
            v_all = qkv[:, nq_a + nk_a:]
            outs["wkp"] = k_all[:n_p].reshape(bp, sp, n_kv, HEAD_DIM_A)[:, sp - WINDOW:][None]
            outs["wvp"] = v_all[:n_p].reshape(bp, sp, n_kv, HEAD_DIM_A)[:, sp - WINDOW:][None]
            outs["wks"] = jnp.concatenate(
                [cache_win_k[0], k_all[n_p:].reshape(bs, ts, n_kv, HEAD_DIM_A)], axis=1)[:, -WINDOW:][None]
            outs["wvs"] = jnp.concatenate(
                [cache_win_v[0], v_all[n_p:].reshape(bs, ts, n_kv, HEAD_DIM_A)], axis=1)[:, -WINDOW:][None]
            w_o = a_w_o[0]
        else:
            zeros_b = jnp.zeros((b_w_qkv.shape[2],), F32)
            qkv = _norm_proj(x, norm_mix[i], b_w_qkv[0].astype(BF16), zeros_b, cos, sin, 0, d)
            o_p = _sb_prompt(qkv, u, bp, sp, n_heads_b)
            o_s = _sb_sample(qkv, cache_sb_k, cache_sb_v, u, n_p, bs, ts, n_heads_b)
            k_all = qkv[:, d:2 * d]
            v_all = qkv[:, 2 * d:]
            outs["skp"] = k_all[:n_p].reshape(1, bp, sp, n_heads_b, HEAD_DIM_B)
            outs["svp"] = v_all[:n_p].reshape(1, bp, sp, n_heads_b, HEAD_DIM_B)
            outs["sks"] = k_all[n_p:].reshape(1, bs, ts, n_heads_b, HEAD_DIM_B)
            outs["svs"] = v_all[n_p:].reshape(1, bs, ts, n_heads_b, HEAD_DIM_B)
            w_o = b_w_o[0]
        wr, br = _router_weights(moe_w_group[i], moe_b_group[i], moe_w_router[i], moe_b_router[i])
        x1, tp, route = _oproj_router(o_p, o_s, x, w_o.astype(BF16), norm_ffn[i], wr, br)
        x = _moe(tp, route, x1, moe_w_gate, moe_w_up, moe_w_down, i, n_p,
                 final_gain=norm_final if i == 1 else None)
    y_p, y_s = x
    return (y_p.reshape(bp, sp, d), y_s.reshape(bs, ts, d),
            outs["wkp"], outs["wvp"], outs["wks"], outs["wvs"],
            outs["skp"], outs["svp"], outs["sks"], outs["svs"])
```

```python
import functools
import math

import jax
import jax.numpy as jnp
from jax import lax
from jax.experimental import pallas as pl
from jax.experimental.pallas import tpu as pltpu

F32 = jnp.float32
BF16 = jnp.bfloat16

CHUNK = 64
WINDOW = 128
HEAD_DIM_A = 64
GROUP_A = 8
HEAD_DIM_B = 128
N_GROUPS = 4
EXPERTS_PER_GROUP = 8
N_EXPERTS = N_GROUPS * EXPERTS_PER_GROUP
ROPE_THETA = 10000.0
RMS_EPS = 1e-6
NEG_INF = -1e30

LANES = 128
ROUTER_LO_LANE = 64
GATE_LANE0 = N_GROUPS
VMEM_LIMIT = 56 * 1024 * 1024

NT_DIMS = (((1,), (1,)), ((), ()))


def _cparams(sem):
    return pltpu.CompilerParams(dimension_semantics=sem, vmem_limit_bytes=VMEM_LIMIT)


def _pick(n, pref, align=8):
    for t in range(min(n, pref), 0, -1):
        if n % t == 0 and t % align == 0:
            return t
    raise ValueError((n, pref, align))


def _norm_proj_kernel(*refs, rope_cols, tn, prompt_tiles):
    if prompt_tiles is None:
        x_ref, g_ref, w_ref, b_ref, cos_ref, sin_ref, o_ref, h_ref = refs
    else:
        x_ref, xs_ref, g_ref, w_ref, b_ref, cos_ref, sin_ref, o_ref, h_ref = refs
    j = pl.program_id(1)

    @pl.when(j == 0)
    def _():
        x = x_ref[...]
        if prompt_tiles is not None:
            x = jnp.where(pl.program_id(0) < prompt_tiles, x, xs_ref[...])
        ms = jnp.mean(x * x, axis=-1, keepdims=True)
        h_ref[...] = (x * lax.rsqrt(ms + RMS_EPS) * g_ref[...]).astype(BF16)

    y = jnp.dot(h_ref[...], w_ref[...], preferred_element_type=F32) + b_ref[...]
    if rope_cols:
        reps = tn // LANES
        cos = jnp.concatenate([cos_ref[...]] * reps, axis=1)
        sin = jnp.concatenate([sin_ref[...]] * reps, axis=1)
        lane = lax.broadcasted_iota(jnp.int32, y.shape, 1)
        half = HEAD_DIM_A // 2
        first = (lane % HEAD_DIM_A) < half
        swapped = jnp.where(first, pltpu.roll(y, tn - half, 1), pltpu.roll(y, half, 1))
        roped = y * cos + swapped * sin
        y = jnp.where(lane + j * tn < rope_cols, roped, y)
    o_ref[...] = y


def _split_tiles(n_p, n_s, pref):
    tm = _pick(math.gcd(n_p, n_s), pref, 16)
    return tm, n_p // tm


def _row_specs(x, pref):
    if not isinstance(x, tuple):
        n, d = x.shape
        tm = _pick(n, pref)
        return [x], [pl.BlockSpec((tm, d), lambda i, *_: (i, 0))], tm, n, None
    x_p, x_s = x
    d = x_p.shape[1]
    tm, prompt_tiles = _split_tiles(x_p.shape[0], x_s.shape[0], pref)
    specs = [pl.BlockSpec((tm, d), lambda i, *_: (jnp.minimum(i, prompt_tiles - 1), 0)),
             pl.BlockSpec((tm, d), lambda i, *_: (jnp.maximum(i - prompt_tiles, 0), 0))]
    return [x_p, x_s], specs, tm, x_p.shape[0] + x_s.shape[0], prompt_tiles


def _norm_proj(x, g, w, b, cos, sin, rope_cols, tn):
    x_args, x_specs, tm, n, prompt_tiles = _row_specs(x, 512)
    d, nout = w.shape
    kern = functools.partial(_norm_proj_kernel, rope_cols=rope_cols, tn=tn, prompt_tiles=prompt_tiles)
    return pl.pallas_call(
        kern,
        grid=(n // tm, nout // tn),
        in_specs=x_specs + [
            pl.BlockSpec((1, d), lambda i, j: (0, 0)),
            pl.BlockSpec((d, tn), lambda i, j: (0, j)),
            pl.BlockSpec((1, tn), lambda i, j: (0, j)),
            pl.BlockSpec((tm, LANES), lambda i, j: (i, 0)),
            pl.BlockSpec((tm, LANES), lambda i, j: (i, 0)),
        ],
        out_specs=pl.BlockSpec((tm, tn), lambda i, j: (i, j)),
        out_shape=jax.ShapeDtypeStruct((n, nout), F32),
        scratch_shapes=[pltpu.VMEM((tm, d), BF16)],
        compiler_params=_cparams(("arbitrary", "arbitrary")),
        name="norm_proj",
    )(*x_args, g.reshape(1, d), w, b.reshape(1, nout), cos, sin)


WIN_KEYS_PAD = 256
HALF = LANES // 2


def _sink_attention(q, k, v, sinks_ref, lo, n_keys, n_kv):
    t = q.shape[0]
    sp = WIN_KEYS_PAD
    assert HEAD_DIM_A == HALF and n_kv % 2 == 0 and n_keys < sp
    qb = (q * (HEAD_DIM_A ** -0.5)).astype(BF16)
    low_half = lax.broadcasted_iota(jnp.int32, (sp, LANES), 1) < HALF
    halves = (low_half, jnp.logical_not(low_half))
    ones = tuple(jnp.where(h, 1.0, 0.0).astype(BF16) for h in halves)
    col = lax.broadcasted_iota(jnp.int32, (1, sp), 1)
    visible = jnp.logical_and(col >= lo, col < n_keys)
    is_sink = col == n_keys

    kmat, vmat = {}, {}
    for slab in range(n_kv // 2):
        ks = k[:, slab * LANES:(slab + 1) * LANES]
        vs = v[:, slab * LANES:(slab + 1) * LANES]
        moved = (pltpu.roll(ks, HALF, 1), pltpu.roll(vs, HALF, 1))
        for gp in range(2):
            for par in range(2):
                src_k, src_v = (ks, vs) if par == gp else moved
                kmat[2 * slab + gp, par] = jnp.where(halves[par], src_k, 0.0).astype(BF16)
                vmat[2 * slab + gp, par] = jnp.concatenate(
                    [jnp.where(halves[par], src_v, 0.0).astype(BF16), ones[par]], axis=1)

    ppg = GROUP_A // 2
    outs = []
    for g in range(n_kv):
        qg = jnp.concatenate([qb[:, (g * ppg + i) * LANES:(g * ppg + i + 1) * LANES] for i in range(ppg)], axis=0)
        acc = None
        for par in range(2):
            s = lax.dot_general(qg, kmat[g, par], NT_DIMS, preferred_element_type=F32)
            s = jnp.where(visible, s, NEG_INF)
            s = jnp.concatenate(
                [jnp.where(is_sink, sinks_ref[2 * (g * ppg + i) + par], s[i * t:(i + 1) * t]) for i in range(ppg)],
                axis=0)
            p = jnp.exp(s - jnp.max(s, axis=-1, keepdims=True)).astype(BF16)
            o = jnp.dot(p, vmat[g, par], preferred_element_type=F32)
            acc = o if acc is None else acc + o
        o_norm = acc[:, :LANES] / acc[:, LANES:]
        outs.extend(o_norm[i * t:(i + 1) * t] for i in range(ppg))
    return jnp.concatenate(outs, axis=1)


def _win_prompt_kernel(sinks_ref, q_ref, k0_ref, k1_ref, k2_ref, v0_ref, v1_ref, v2_ref, o_ref, *, n_kv):
    c = pl.program_id(1)
    pad = jnp.zeros((WIN_KEYS_PAD - 3 * CHUNK, k0_ref.shape[1]), F32)
    k = jnp.concatenate([k0_ref[...], k1_ref[...], k2_ref[...], pad], axis=0)
    v = jnp.concatenate([v0_ref[...], v1_ref[...], v2_ref[...], pad], axis=0)
    lo = jnp.maximum(2 - c, 0) * CHUNK
    o_ref[...] = _sink_attention(q_ref[...], k, v, sinks_ref, lo, 3 * CHUNK, n_kv).astype(o_ref.dtype)


def _win_prompt(qkv, sinks, batch, seq, n_kv):
    nq = n_kv * GROUP_A * HEAD_DIM_A
    nk = n_kv * HEAD_DIM_A
    nc = seq // CHUNK
    kcol = nq // nk
    q_spec = pl.BlockSpec((CHUNK, nq), lambda b, c: (b * nc + c, 0))

    def kv_spec(back, col):
        return pl.BlockSpec((CHUNK, nk), lambda b, c: (b * nc + jnp.maximum(c - back, 0), col))

    return pl.pallas_call(
        functools.partial(_win_prompt_kernel, n_kv=n_kv),
        grid=(batch, nc),
        in_specs=[pl.BlockSpec(memory_space=pltpu.SMEM),
                  q_spec, kv_spec(2, kcol), kv_spec(1, kcol), kv_spec(0, kcol),
                  kv_spec(2, kcol + 1), kv_spec(1, kcol + 1), kv_spec(0, kcol + 1)],
        out_specs=pl.BlockSpec((CHUNK, nq), lambda b, c: (b * nc + c, 0)),
        out_shape=jax.ShapeDtypeStruct((batch * seq, nq), BF16),
        compiler_params=_cparams(("arbitrary", "arbitrary")),
        name="win_prompt",
    )(sinks, qkv, qkv, qkv, qkv, qkv, qkv, qkv)


def _win_sample_kernel(sinks_ref, q_ref, kn_ref, vn_ref, ck_ref, cv_ref, o_ref, *, n_kv):
    t = kn_ref.shape[0]
    pad = jnp.zeros((WIN_KEYS_PAD - WINDOW - t, kn_ref.shape[1]), F32)
    k = jnp.concatenate([ck_ref[0], kn_ref[...], pad], axis=0)
    v = jnp.concatenate([cv_ref[0], vn_ref[...], pad], axis=0)
    o_ref[...] = _sink_attention(q_ref[...], k, v, sinks_ref, 0, WINDOW + t, n_kv).astype(o_ref.dtype)


def _win_sample(qkv, cache_k, cache_v, sinks, row0, batch, t, n_kv):
    nq = n_kv * GROUP_A * HEAD_DIM_A
    nk = n_kv * HEAD_DIM_A
    kcol = nq // nk
    blk0 = row0 // t
    return pl.pallas_call(
        functools.partial(_win_sample_kernel, n_kv=n_kv),
        grid=(batch,),
        in_specs=[pl.BlockSpec(memory_space=pltpu.SMEM),
                  pl.BlockSpec((t, nq), lambda b: (blk0 + b, 0)),
                  pl.BlockSpec((t, nk), lambda b: (blk0 + b, kcol)),
                  pl.BlockSpec((t, nk), lambda b: (blk0 + b, kcol + 1)),
                  pl.BlockSpec((1, WINDOW, nk), lambda b: (b, 0, 0)),
                  pl.BlockSpec((1, WINDOW, nk), lambda b: (b, 0, 0))],
        out_specs=pl.BlockSpec((t, nq), lambda b: (b, 0)),
        out_shape=jax.ShapeDtypeStruct((batch * t, nq), BF16),
        compiler_params=_cparams(("arbitrary",)),
        name="win_sample",
    )(sinks, qkv, qkv, qkv, cache_k, cache_v)


SB_STOP_BELOW = -110.0
SB_BLOCK = 256


def _sb_weights(z, mask, carry, u):
    sp = jnp.maximum(z, 0.0) + jnp.log(1.0 + jnp.exp(-jnp.abs(z)))
    ls = -sp if mask is None else jnp.where(mask, -sp, 0.0)
    hi = ls.astype(BF16)
    lo = (ls - hi.astype(F32)).astype(BF16)
    after = (jnp.dot(hi, u, preferred_element_type=F32) + jnp.dot(lo, u, preferred_element_type=F32)) + carry
    a = jnp.exp(z - sp + after)
    if mask is not None:
        a = jnp.where(mask, a, 0.0)
    return a, carry + jnp.sum(ls, axis=-1, keepdims=True)


def _sb_prompt_kernel(q_ref, k_ref, v_ref, u_ref, o_ref, *, seq):
    tb = SB_BLOCK
    scale = HEAD_DIM_B ** -0.5
    u = u_ref[...]
    row = lax.broadcasted_iota(jnp.int32, (tb, tb), 0)
    col = lax.broadcasted_iota(jnp.int32, (tb, tb), 1)
    strictly_earlier = col < row
    no_carry = jnp.zeros((tb, 1), F32)

    def queries(start):
        return (q_ref[pl.ds(start, tb), :] * scale).astype(BF16)

    def scores(q, start, size):
        kb = k_ref[pl.ds(start, size), :].astype(BF16)
        return lax.dot_general(q, kb, NT_DIMS, preferred_element_type=F32)

    def weighted_values(a, start, size):
        vb = v_ref[pl.ds(start, size), :].astype(BF16)
        return jnp.dot(a.astype(BF16), vb, preferred_element_type=F32)

    a, _ = _sb_weights(scores(queries(0), 0, tb), strictly_earlier, no_carry, u)
    o_ref[pl.ds(0, tb), :] = weighted_values(a, 0, tb).astype(o_ref.dtype)

    def q_block(qi, _):
        q0 = pl.multiple_of(qi * tb, tb)
        p0 = pl.multiple_of(q0 - tb, tb)
        q = queries(q0)
        z = scores(q, p0, 2 * tb)
        a_own, carry = _sb_weights(z[:, tb:], strictly_earlier, no_carry, u)
        a_prev, carry = _sb_weights(z[:, :tb], None, carry, u)
        acc = weighted_values(jnp.concatenate([a_prev, a_own], axis=1), p0, 2 * tb)

        def more(state):
            kj, cmax, _, _ = state
            return jnp.logical_and(kj >= 0, cmax > SB_STOP_BELOW)

        def earlier_block(state):
            kj, _, carry, acc = state
            k0 = pl.multiple_of(kj * tb, tb)
            a, carry = _sb_weights(scores(q, k0, tb), None, carry, u)
            return kj - 1, jnp.max(carry), carry, acc + weighted_values(a, k0, tb)

        _, _, _, acc = lax.while_loop(more, earlier_block, (qi - 2, jnp.max(carry), carry, acc))
        o_ref[pl.ds(q0, tb), :] = acc.astype(o_ref.dtype)
        return 0

    lax.fori_loop(1, seq // tb, q_block, 0)


def _sb_prompt(qkv, u, batch, seq, n_heads):
    hd = HEAD_DIM_B
    assert seq % SB_BLOCK == 0
    return pl.pallas_call(
        functools.partial(_sb_prompt_kernel, seq=seq),
        grid=(batch, n_heads),
        in_specs=[pl.BlockSpec((seq, hd), lambda b, h: (b, h)),
                  pl.BlockSpec((seq, hd), lambda b, h: (b, n_heads + h)),
                  pl.BlockSpec((seq, hd), lambda b, h: (b, 2 * n_heads + h)),
                  pl.BlockSpec((SB_BLOCK, SB_BLOCK), lambda b, h: (0, 0))],
        out_specs=pl.BlockSpec((seq, hd), lambda b, h: (b, h)),
        out_shape=jax.ShapeDtypeStruct((batch * seq, n_heads * hd), BF16),
        compiler_params=_cparams(("arbitrary", "arbitrary")),
        name="sb_prompt",
    )(qkv, qkv, qkv, u)


SB_NEW_PAD = SB_BLOCK


def _sb_sample_kernel(q_ref, kn_ref, vn_ref, u_ref, ck_ref, cv_ref, o_ref, kbuf, vbuf, sem,
                      *, t, n_heads, n_blocks):
    b = pl.program_id(0)
    tb = SB_BLOCK
    hd = HEAD_DIM_B
    d = n_heads * hd
    rows = n_heads * t
    scale = hd ** -0.5
    u = u_ref[...]

    def block_copies(j, slot):
        start = pl.multiple_of((n_blocks - 1 - j) * tb, tb)
        out = []
        for h in range(n_heads):
            out.append(pltpu.make_async_copy(ck_ref.at[0, b, pl.ds(start, tb), h, :], kbuf.at[slot, h],
                                             sem.at[slot, 0]))
            out.append(pltpu.make_async_copy(cv_ref.at[0, b, pl.ds(start, tb), h, :], vbuf.at[slot, h],
                                             sem.at[slot, 1]))
        return out

    for c in block_copies(0, 0):
        c.start()

    q = q_ref[...] * scale
    qh = [q[:, h * hd:(h + 1) * hd].astype(BF16) for h in range(n_heads)]

    pad = jnp.zeros((SB_NEW_PAD - t, d), F32)
    kn = jnp.concatenate([kn_ref[...], pad], axis=0).astype(BF16)
    vn = jnp.concatenate([vn_ref[...], pad], axis=0).astype(BF16)
    z = jnp.concatenate(
        [lax.dot_general(qh[h], kn[:, h * hd:(h + 1) * hd], NT_DIMS, preferred_element_type=F32)
         for h in range(n_heads)], axis=0)
    rr = lax.broadcasted_iota(jnp.int32, (rows, SB_NEW_PAD), 0) % t
    cc = lax.broadcasted_iota(jnp.int32, (rows, SB_NEW_PAD), 1)
    a, carry = _sb_weights(z, cc < rr, jnp.zeros((rows, 1), F32), u[:SB_NEW_PAD, :SB_NEW_PAD])
    a = a.astype(BF16)
    acc = jnp.concatenate(
        [jnp.dot(a[h * t:(h + 1) * t], vn[:, h * hd:(h + 1) * hd], preferred_element_type=F32)
         for h in range(n_heads)], axis=1)

    def more(state):
        j, cmax, _, _ = state
        return jnp.logical_and(j < n_blocks, cmax > SB_STOP_BELOW)

    def cache_block(state):
        j, _, carry, acc = state
        slot = j % 2
        for c in block_copies(j, slot):
            c.wait()

        @pl.when(j + 1 < n_blocks)
        def _():
            for c in block_copies(j + 1, 1 - slot):
                c.start()

        z = jnp.concatenate(
            [lax.dot_general(qh[h], kbuf[slot, h].astype(BF16), NT_DIMS, preferred_element_type=F32)
             for h in range(n_heads)], axis=0)
        a, carry = _sb_weights(z, None, carry, u)
        a = a.astype(BF16)
        out = jnp.concatenate(
            [jnp.dot(a[h * t:(h + 1) * t], vbuf[slot, h].astype(BF16), preferred_element_type=F32)
             for h in range(n_heads)], axis=1)
        return j + 1, jnp.max(carry), carry, acc + out

    j, _, _, acc = lax.while_loop(more, cache_block, (jnp.int32(0), jnp.max(carry), carry, acc))
    o_ref[...] = acc.astype(o_ref.dtype)

    @pl.when(j < n_blocks)
    def _():
        for c in block_copies(j, j % 2):
            c.wait()


def _sb_sample(qkv, cache_k, cache_v, u, row0, batch, t, n_heads):
    hd = HEAD_DIM_B
    d = n_heads * hd
    past = cache_k.shape[2]
    assert past % SB_BLOCK == 0 and t <= SB_NEW_PAD
    blk0 = row0 // t
    return pl.pallas_call(
        functools.partial(_sb_sample_kernel, t=t, n_heads=n_heads, n_blocks=past // SB_BLOCK),
        grid=(batch,),
        in_specs=[pl.BlockSpec((t, d), lambda b: (blk0 + b, 0)),
                  pl.BlockSpec((t, d), lambda b: (blk0 + b, 1)),
                  pl.BlockSpec((t, d), lambda b: (blk0 + b, 2)),
                  pl.BlockSpec((SB_BLOCK, SB_BLOCK), lambda b: (0, 0)),
                  pl.BlockSpec(memory_space=pl.ANY),
                  pl.BlockSpec(memory_space=pl.ANY)],
        out_specs=pl.BlockSpec((t, d), lambda b: (b, 0)),
        out_shape=jax.ShapeDtypeStruct((batch * t, d), BF16),
        scratch_shapes=[pltpu.VMEM((2, n_heads, SB_BLOCK, hd), F32),
                        pltpu.VMEM((2, n_heads, SB_BLOCK, hd), F32),
                        pltpu.SemaphoreType.DMA((2, 2))],
        compiler_params=_cparams(("arbitrary",)),
        name="sb_sample",
    )(qkv, qkv, qkv, u, cache_k, cache_v)


def _pack_bf16_pair(a, b):
    ab = lax.bitcast_convert_type(a.astype(BF16).astype(F32), jnp.uint32)
    bb = lax.bitcast_convert_type(b.astype(BF16).astype(F32), jnp.uint32)
    return (ab >> 16) | (bb & jnp.uint32(0xFFFF0000))


def _unpack_bf16_pair(p):
    a = lax.bitcast_convert_type(p << 16, F32)
    b = lax.bitcast_convert_type(p & jnp.uint32(0xFFFF0000), F32)
    return a.astype(BF16), b.astype(BF16)


def _oproj_router_kernel(op_ref, os_ref, *refs, prompt_tiles, split_x):
    if split_x:
        x_ref, xs_ref, wo_ref, g_ref, wr_ref, br_ref, x1_ref, tp_ref, route_ref = refs
    else:
        x_ref, wo_ref, g_ref, wr_ref, br_ref, x1_ref, tp_ref, route_ref = refs
    in_prompt = pl.program_id(0) < prompt_tiles
    o = jnp.where(in_prompt, op_ref[...], os_ref[...])
    x = x_ref[...]
    if split_x:
        x = jnp.where(in_prompt, x, xs_ref[...])
    x1 = x + jnp.dot(o, wo_ref[...], preferred_element_type=F32)
    x1_ref[...] = x1
    ms = jnp.mean(x1 * x1, axis=-1, keepdims=True)
    t = x1 * lax.rsqrt(ms + RMS_EPS) * g_ref[...]
    half = t.shape[1] // 2
    tp_ref[...] = _pack_bf16_pair(t[:, :half], t[:, half:])
    hi = t.astype(BF16)
    lo = (t - hi.astype(F32)).astype(BF16)
    r_hi = jnp.dot(hi, wr_ref[...], preferred_element_type=F32)
    r_lo = jnp.dot(lo, wr_ref[...], preferred_element_type=F32)
    logits = r_hi + pltpu.roll(r_hi, LANES - ROUTER_LO_LANE, 1) + r_lo + br_ref[...]

    lane = lax.broadcasted_iota(jnp.int32, logits.shape, 1)
    lane_f = lane.astype(F32)
    big = float(LANES)
    is_group = lane < N_GROUPS
    gl = jnp.where(is_group, logits, -jnp.inf)
    gmax = jnp.max(gl, axis=-1, keepdims=True)
    gidx = jnp.min(jnp.where(gl == gmax, lane_f, big), axis=-1, keepdims=True)
    gsum = jnp.sum(jnp.where(is_group, jnp.exp(logits - gmax), 0.0), axis=-1, keepdims=True)
    g_w = 1.0 / gsum
    expert = lane - GATE_LANE0
    in_group = (expert >= 0) & (expert < N_EXPERTS) & ((expert // EXPERTS_PER_GROUP).astype(F32) == gidx)
    el = jnp.where(in_group, logits, -jnp.inf)
    v1 = jnp.max(el, axis=-1, keepdims=True)
    i1 = jnp.min(jnp.where(el == v1, lane_f, big), axis=-1, keepdims=True)
    el2 = jnp.where(lane_f == i1, -jnp.inf, el)
    v2 = jnp.max(el2, axis=-1, keepdims=True)
    i2 = jnp.min(jnp.where(el2 == v2, lane_f, big), axis=-1, keepdims=True)
    e21 = jnp.exp(v2 - v1)
    w1 = g_w / (1.0 + e21)
    w2 = g_w * e21 / (1.0 + e21)
    route_ref[...] = jnp.where(lane == 0, i1 - GATE_LANE0,
                               jnp.where(lane == 1, i2 - GATE_LANE0,
                                         jnp.where(lane == 2, w1, jnp.where(lane == 3, w2, 0.0))))


def _oproj_router(o_p, o_s, x, wo, g, wr, br):
    d = wo.shape[0]
    o_args, o_specs, tm, n, prompt_tiles = _row_specs((o_p, o_s), 256)
    if isinstance(x, tuple):
        x_args, x_specs = list(x), list(o_specs)
    else:
        x_args, x_specs = [x], [pl.BlockSpec((tm, d), lambda i: (i, 0))]
    row = lambda i: (i, 0)
    const = lambda i: (0, 0)
    return pl.pallas_call(
        functools.partial(_oproj_router_kernel, prompt_tiles=prompt_tiles, split_x=isinstance(x, tuple)),
        grid=(n // tm,),
        in_specs=o_specs + x_specs + [
                  pl.BlockSpec((d, d), const), pl.BlockSpec((1, d), const),
                  pl.BlockSpec((d, LANES), const), pl.BlockSpec((1, LANES), const)],
        out_specs=[pl.BlockSpec((tm, d), row), pl.BlockSpec((tm, d // 2), row), pl.BlockSpec((tm, LANES), row)],
        out_shape=[jax.ShapeDtypeStruct((n, d), F32), jax.ShapeDtypeStruct((n, d // 2), jnp.uint32),
                   jax.ShapeDtypeStruct((n, LANES), F32)],
        compiler_params=_cparams(("arbitrary",)),
        name="oproj_router",
    )(*o_args, *x_args, wo, g.reshape(1, d), wr, br)


MOE_TILE = 256
TOP_K = 2


def _moe_plan(route, n_tiles):
    n = route.shape[0]
    e_flat = route[:, :TOP_K].astype(jnp.int32).reshape(n * TOP_K)
    onehot = (e_flat[:, None] == jnp.arange(N_EXPERTS, dtype=jnp.int32)[None, :]).astype(jnp.int32)
    csum = jnp.cumsum(onehot, axis=0)
    rank = jnp.sum(csum * onehot, axis=1) - 1
    counts = csum[-1]
    tiles = (counts + MOE_TILE - 1) // MOE_TILE
    tile_end = jnp.cumsum(tiles)
    row_start = (tile_end - tiles) * MOE_TILE
    pos = jnp.sum(onehot * row_start[None, :], axis=1) + rank
    n_used = tile_end[-1]
    tile_id = jnp.minimum(jnp.arange(n_tiles, dtype=jnp.int32), n_used - 1)
    tile_expert = jnp.sum((tile_end[None, :] <= tile_id[:, None]).astype(jnp.int32), axis=1)
    return pos.astype(jnp.int32), tile_expert.astype(jnp.int32), n_used.reshape(1).astype(jnp.int32)


def _dispatch_kernel(pos_ref, tp_ref, xs_in_ref, xs_ref, sem, *, tc):
    del xs_in_ref
    base = pl.program_id(0) * tc

    def issue(j, carry):
        for s in range(TOP_K):
            p = pos_ref[TOP_K * (base + j) + s]
            pltpu.make_async_copy(tp_ref.at[pl.ds(j, 1)], xs_ref.at[pl.ds(p, 1)], sem).start()
        return carry

    lax.fori_loop(0, tc, issue, 0)
    for _ in range(TOP_K):
        pltpu.make_async_copy(tp_ref, xs_ref.at[pl.ds(0, tc)], sem).wait()


def _dispatch(pos, tp, n_rows_pad):
    n, dw = tp.shape
    tc = _pick(n, 1536)
    assert tc <= n_rows_pad
    grid_spec = pltpu.PrefetchScalarGridSpec(
        num_scalar_prefetch=1, grid=(n // tc,),
        in_specs=[pl.BlockSpec((tc, dw), lambda i, pos: (i, 0)), pl.BlockSpec(memory_space=pl.ANY)],
        out_specs=pl.BlockSpec(memory_space=pl.ANY),
        scratch_shapes=[pltpu.SemaphoreType.DMA(())])
    return pl.pallas_call(
        functools.partial(_dispatch_kernel, tc=tc),
        grid_spec=grid_spec,
        out_shape=jax.ShapeDtypeStruct((n_rows_pad, dw), jnp.uint32),
        input_output_aliases={2: 0},
        compiler_params=_cparams(("arbitrary",)),
        name="moe_dispatch",
    )(pos, tp, jnp.zeros((n_rows_pad, dw), jnp.uint32))


def _experts_kernel(te_ref, nu_ref, xs_ref, wg_ref, wu_ref, wd_ref, ys_ref, wg_s, wu_s, wd_s):
    i = pl.program_id(0)
    used = i < nu_ref[0]
    new_expert = jnp.logical_or(i == 0, te_ref[i] != te_ref[jnp.maximum(i - 1, 0)])

    @pl.when(jnp.logical_and(used, new_expert))
    def _():
        wg_s[...] = wg_ref[0, 0].astype(BF16)
        wu_s[...] = wu_ref[0, 0].astype(BF16)
        wd_s[...] = wd_ref[0, 0].astype(BF16)

    @pl.when(jnp.logical_not(used))
    def _():
        ys_ref[...] = jnp.zeros_like(ys_ref)

    @pl.when(used)
    def _():
        a, b = _unpack_bf16_pair(xs_ref[...])
        x = jnp.concatenate([a, b], axis=1)
        hg = jnp.dot(x, wg_s[...], preferred_element_type=F32)
        hu = jnp.dot(x, wu_s[...], preferred_element_type=F32)
        hid = (hg * jax.nn.sigmoid(hg) * hu).astype(BF16)
        ys_ref[...] = jnp.dot(hid, wd_s[...], preferred_element_type=F32)


def _experts(tile_expert, n_used, xs, wg, wu, wd, layer):
    rows, dw = xs.shape
    _, _, d, de = wg.shape
    grid_spec = pltpu.PrefetchScalarGridSpec(
        num_scalar_prefetch=2, grid=(rows // MOE_TILE,),
        in_specs=[pl.BlockSpec((MOE_TILE, dw), lambda i, te, nu: (i, 0)),
                  pl.BlockSpec((1, 1, d, de), lambda i, te, nu: (layer, te[i], 0, 0)),
                  pl.BlockSpec((1, 1, d, de), lambda i, te, nu: (layer, te[i], 0, 0)),
                  pl.BlockSpec((1, 1, de, d), lambda i, te, nu: (layer, te[i], 0, 0))],
        out_specs=pl.BlockSpec((MOE_TILE, d), lambda i, te, nu: (i, 0)),
        scratch_shapes=[pltpu.VMEM((d, de), BF16), pltpu.VMEM((d, de), BF16), pltpu.VMEM((de, d), BF16)])
    return pl.pallas_call(
        _experts_kernel,
        grid_spec=grid_spec,
        out_shape=jax.ShapeDtypeStruct((rows, d), F32),
        compiler_params=_cparams(("arbitrary",)),
        name="moe_experts",
    )(tile_expert, n_used, xs, wg, wu, wd)


COMBINE_ROWS = 32


def _combine_kernel(pos_ref, ys_ref, x1_ref, route_ref, *rest, tc, prompt_tiles):
    if prompt_tiles is None:
        o_ref, buf_ref, sem = rest
    else:
        g_ref, yp_ref, ysm_ref, buf_ref, sem = rest
    i = pl.program_id(0)
    n_steps = pl.num_programs(0)
    slot = i % 2

    def issue_rows(step, to_slot, row0, n_rows):
        for r in range(n_rows):
            for s in range(TOP_K):
                p = pos_ref[TOP_K * (step * tc + row0 + r) + s]
                pltpu.make_async_copy(ys_ref.at[pl.ds(p, 1)], buf_ref.at[to_slot, s, pl.ds(row0 + r, 1)],
                                      sem.at[to_slot]).start()

    @pl.when(i == 0)
    def _():
        lax.fori_loop(0, tc // COMBINE_ROWS,
                      lambda c, carry: (issue_rows(0, 0, c * COMBINE_ROWS, COMBINE_ROWS), carry)[1], 0)

    for s in range(TOP_K):
        pltpu.make_async_copy(ys_ref.at[pl.ds(0, tc)], buf_ref.at[slot, s], sem.at[slot]).wait()

    def run(prefetch, out_ref):
        def body(c, carry):
            r0 = pl.multiple_of(c * COMBINE_ROWS, COMBINE_ROWS)
            if prefetch:
                issue_rows(i + 1, 1 - slot, r0, COMBINE_ROWS)
            rows = pl.ds(r0, COMBINE_ROWS)
            route = route_ref[rows, :]
            x = (x1_ref[rows, :] + buf_ref[slot, 0, rows, :] * route[:, TOP_K:TOP_K + 1]
                 + buf_ref[slot, 1, rows, :] * route[:, TOP_K + 1:TOP_K + 2])
            if prompt_tiles is not None:
                ms = jnp.mean(x * x, axis=-1, keepdims=True)
                x = x * lax.rsqrt(ms + RMS_EPS) * g_ref[...]
            out_ref[rows, :] = x
            return carry
        lax.fori_loop(0, tc // COMBINE_ROWS, body, 0)

    more = i + 1 < n_steps
    for prefetch, cond in ((True, more), (False, jnp.logical_not(more))):
        if prompt_tiles is None:
            pl.when(cond)(functools.partial(run, prefetch, o_ref))
        else:
            pl.when(jnp.logical_and(cond, i < prompt_tiles))(functools.partial(run, prefetch, yp_ref))
            pl.when(jnp.logical_and(cond, i >= prompt_tiles))(functools.partial(run, prefetch, ysm_ref))


def _combine(pos, ys, x1, route, n_p, final_gain=None):
    n, d = x1.shape
    tc, prompt_tiles = _split_tiles(n_p, n - n_p, 256)
    assert tc % COMBINE_ROWS == 0
    row = lambda i, pos: (i, 0)
    in_specs = [pl.BlockSpec(memory_space=pl.ANY), pl.BlockSpec((tc, d), row), pl.BlockSpec((tc, LANES), row)]
    args = [pos, ys, x1, route]
    if final_gain is None:
        out_specs = pl.BlockSpec((tc, d), row)
        out_shape = jax.ShapeDtypeStruct((n, d), F32)
    else:
        in_specs.append(pl.BlockSpec((1, d), lambda i, pos: (0, 0)))
        args.append(final_gain.reshape(1, d))
        out_specs = [pl.BlockSpec((tc, d), lambda i, pos: (jnp.minimum(i, prompt_tiles - 1), 0)),
                     pl.BlockSpec((tc, d), lambda i, pos: (jnp.maximum(i - prompt_tiles, 0), 0))]
        out_shape = [jax.ShapeDtypeStruct((n_p, d), F32), jax.ShapeDtypeStruct((n - n_p, d), F32)]
    grid_spec = pltpu.PrefetchScalarGridSpec(
        num_scalar_prefetch=1, grid=(n // tc,), in_specs=in_specs, out_specs=out_specs,
        scratch_shapes=[pltpu.VMEM((2, TOP_K, tc, d), F32), pltpu.SemaphoreType.DMA((2,))])
    return pl.pallas_call(
        functools.partial(_combine_kernel, tc=tc, prompt_tiles=None if final_gain is None else prompt_tiles),
        grid_spec=grid_spec,
        out_shape=out_shape,
        compiler_params=_cparams(("arbitrary",)),
        name="moe_combine",
    )(*args)


def _moe(tp, route, x1, wg, wu, wd, layer, n_p, final_gain=None):
    n = x1.shape[0]
    n_tiles = -(-n * TOP_K // MOE_TILE) + N_EXPERTS
    pos, tile_expert, n_used = _moe_plan(route, n_tiles)
    xs = _dispatch(pos, tp, n_tiles * MOE_TILE)
    ys = _experts(tile_expert, n_used, xs, wg, wu, wd, layer)
    return _combine(pos, ys, x1, route, n_p, final_gain)


def _rope_tables(pos):
    half = HEAD_DIM_A // 2
    inv = ROPE_THETA ** (-jnp.arange(half, dtype=F32) / half)
    ang = pos.astype(F32)[:, None] * inv[None, :]
    cos = jnp.cos(ang)
    sin = jnp.sin(ang)
    reps = LANES // HEAD_DIM_A
    return jnp.tile(jnp.concatenate([cos, cos], axis=1), (1, reps)), \
        jnp.tile(jnp.concatenate([-sin, sin], axis=1), (1, reps))


def _router_weights(w_group, b_group, w_router, b_router):
    d = w_group.shape[0]
    w = jnp.concatenate([w_group, w_router], axis=1).astype(F32)
    n_log = w.shape[1]
    hi = w.astype(BF16)
    lo = (w - hi.astype(F32)).astype(BF16)
    wr = jnp.zeros((d, LANES), BF16)
    wr = wr.at[:, :n_log].set(hi).at[:, ROUTER_LO_LANE:ROUTER_LO_LANE + n_log].set(lo)
    br = jnp.zeros((1, LANES), F32).at[0, :n_log].set(jnp.concatenate([b_group, b_router]).astype(F32))
    return wr, br


def kernel(x_prompt, x_sample, cache_win_k, cache_win_v, cache_sb_k, cache_sb_v, norm_mix, norm_ffn, norm_final,
           a_w_qkv, a_b_qkv, a_sinks, a_w_o, b_w_qkv, b_w_o, moe_w_group, moe_b_group, moe_w_router,
           moe_b_router, moe_w_gate, moe_w_up, moe_w_down):
    bp, sp, d = x_prompt.shape
    bs, ts, _ = x_sample.shape
    n_p = bp * sp
    n_s = bs * ts
    past = cache_sb_k.shape[2]
    n_kv = cache_win_k.shape[3]
    n_heads_b = cache_sb_k.shape[3]
    nq_a = n_kv * GROUP_A * HEAD_DIM_A
    nk_a = n_kv * HEAD_DIM_A
    assert sp % CHUNK == 0 and past % CHUNK == 0 and ts <= CHUNK and n_p % ts == 0
    assert cache_win_k.shape[2] == WINDOW and d == n_heads_b * HEAD_DIM_B == nq_a

    x = (x_prompt.reshape(n_p, d), x_sample.reshape(n_s, d))
    pos = jnp.concatenate([jnp.tile(jnp.arange(sp, dtype=jnp.int32), bp),
                           jnp.tile(past + jnp.arange(ts, dtype=jnp.int32), bs)])
    cos, sin = _rope_tables(pos)
    u = (lax.broadcasted_iota(jnp.int32, (SB_BLOCK, SB_BLOCK), 0)
         > lax.broadcasted_iota(jnp.int32, (SB_BLOCK, SB_BLOCK), 1)).astype(BF16)

    outs = {}
    for i in range(2):
        if i == 0:
            tn = _pick(a_w_qkv.shape[2], 1280)
            qkv = _norm_proj(x, norm_mix[i], a_w_qkv[0].astype(BF16), a_b_qkv[0], cos, sin, nq_a + nk_a, tn)
            sinks = a_sinks[0].astype(F32)
            o_p = _win_prompt(qkv, sinks, bp, sp, n_kv)
            o_s = _win_sample(qkv, cache_win_k[0].reshape(bs, WINDOW, nk_a), cache_win_v[0].reshape(bs, WINDOW, nk_a),
                              sinks, n_p, bs, ts, n_kv)
            k_all = qkv[:, nq_a:nq_a + nk_a]
            v_all = qkv[:, nq_a + nk_a:]
            outs["wkp"] = k_all[:n_p].reshape(bp, sp, n_kv, HEAD_DIM_A)[:, sp - WINDOW:][None]
            outs["wvp"] = v_all[:n_p].reshape(bp, sp, n_kv, HEAD_DIM_A)[:, sp - WINDOW:][None]
            outs["wks"] = jnp.concatenate(
                [cache_win_k[0], k_all[n_p:].reshape(bs, ts, n_kv, HEAD_DIM_A)], axis=1)[:, -WINDOW:][None]
            outs["wvs"] = jnp.concatenate(
                [cache_win_v[0], v_all[n_p:].reshape(bs, ts, n_kv, HEAD_DIM_A)], axis=1)[:, -WINDOW:][None]
            w_o = a_w_o[0]
        else:
            zeros_b = jnp.zeros((b_w_qkv.shape[2],), F32)
            qkv = _norm_proj(x, norm_mix[i], b_w_qkv[0].astype(BF16), zeros_b, cos, sin, 0, d)
            o_p = _sb_prompt(qkv, u, bp, sp, n_heads_b)
            o_s = _sb_sample(qkv, cache_sb_k, cache_sb_v, u, n_p, bs, ts, n_heads_b)
            k_all = qkv[:, d:2 * d]
            v_all = qkv[:, 2 * d:]
            outs["skp"] = k_all[:n_p].reshape(1, bp, sp, n_heads_b, HEAD_DIM_B)
            outs["svp"] = v_all[:n_p].reshape(1, bp, sp, n_heads_b, HEAD_DIM_B)
            outs["sks"] = k_all[n_p:].reshape(1, bs, ts, n_heads_b, HEAD_DIM_B)
            outs["svs"] = v_all[n_p:].reshape(1, bs, ts, n_heads_b, HEAD_DIM_B)
            w_o = b_w_o[0]
        wr, br = _router_weights(moe_w_group[i], moe_b_group[i], moe_w_router[i], moe_b_router[i])
        x1, tp, route = _oproj_router(o_p, o_s, x, w_o.astype(BF16), norm_ffn[i], wr, br)
        x = _moe(tp, route, x1, moe_w_gate, moe_w_up, moe_w_down, i, n_p,
                 final_gain=norm_final if i == 1 else None)
    y_p, y_s = x
    return (y_p.reshape(bp, sp, d), y_s.reshape(bs, ts, d),
            outs["wkp"], outs["wvp"], outs["wks"], outs["wvs"],
            outs["skp"], outs["svp"], outs["sks"], outs["svs"])
```

```python
import functools
import math

import jax
import jax.numpy as jnp
from jax import lax
from jax.experimental import pallas as pl
from jax.experimental.pallas import tpu as pltpu

F32 = jnp.float32
BF16 = jnp.bfloat16

CHUNK = 64
WINDOW = 128
HEAD_DIM_A = 64
GROUP_A = 8
HEAD_DIM_B = 128
N_GROUPS = 4
EXPERTS_PER_GROUP = 8
N_EXPERTS = N_GROUPS * EXPERTS_PER_GROUP
ROPE_THETA = 10000.0
RMS_EPS = 1e-6
NEG_INF = -1e30

LANES = 128
ROUTER_LO_LANE = 64
GATE_LANE0 = N_GROUPS
VMEM_LIMIT = 56 * 1024 * 1024

NT_DIMS = (((1,), (1,)), ((), ()))


def _cparams(sem):
    return pltpu.CompilerParams(dimension_semantics=sem, vmem_limit_bytes=VMEM_LIMIT)


def _pick(n, pref, align=8):
    for t in range(min(n, pref), 0, -1):
        if n % t == 0 and t % align == 0:
            return t
    raise ValueError((n, pref, align))


def _norm_proj_kernel(*refs, rope_cols, tn, prompt_tiles):
    if prompt_tiles is None:
        x_ref, g_ref, w_ref, b_ref, cos_ref, sin_ref, o_ref, h_ref = refs
    else:
        x_ref, xs_ref, g_ref, w_ref, b_ref, cos_ref, sin_ref, o_ref, h_ref = refs
    j = pl.program_id(1)

    @pl.when(j == 0)
    def _():
        x = x_ref[...]
        if prompt_tiles is not None:
            x = jnp.where(pl.program_id(0) < prompt_tiles, x, xs_ref[...])
        ms = jnp.mean(x * x, axis=-1, keepdims=True)
        h_ref[...] = (x * lax.rsqrt(ms + RMS_EPS) * g_ref[...]).astype(BF16)

    y = jnp.dot(h_ref[...], w_ref[...], preferred_element_type=F32) + b_ref[...]
    if rope_cols:
        reps = tn // LANES
        cos = jnp.concatenate([cos_ref[...]] * reps, axis=1)
        sin = jnp.concatenate([sin_ref[...]] * reps, axis=1)
        lane = lax.broadcasted_iota(jnp.int32, y.shape, 1)
        half = HEAD_DIM_A // 2
        first = (lane % HEAD_DIM_A) < half
        swapped = jnp.where(first, pltpu.roll(y, tn - half, 1), pltpu.roll(y, half, 1))
        roped = y * cos + swapped * sin
        y = jnp.where(lane + j * tn < rope_cols, roped, y)
    o_ref[...] = y


def _split_tiles(n_p, n_s, pref):
    tm = _pick(math.gcd(n_p, n_s), pref, 16)
    return tm, n_p // tm


def _row_specs(x, pref):
    if not isinstance(x, tuple):
        n, d = x.shape
        tm = _pick(n, pref)
        return [x], [pl.BlockSpec((tm, d), lambda i, *_: (i, 0))], tm, n, None
    x_p, x_s = x
    d = x_p.shape[1]
    tm, prompt_tiles = _split_tiles(x_p.shape[0], x_s.shape[0], pref)
    specs = [pl.BlockSpec((tm, d), lambda i, *_: (jnp.minimum(i, prompt_tiles - 1), 0)),
             pl.BlockSpec((tm, d), lambda i, *_: (jnp.maximum(i - prompt_tiles, 0), 0))]
    return [x_p, x_s], specs, tm, x_p.shape[0] + x_s.shape[0], prompt_tiles


def _norm_proj(x, g, w, b, cos, sin, rope_cols, tn):
    x_args, x_specs, tm, n, prompt_tiles = _row_specs(x, 512)
    d, nout = w.shape
    kern = functools.partial(_norm_proj_kernel, rope_cols=rope_cols, tn=tn, prompt_tiles=prompt_tiles)
    return pl.pallas_call(
        kern,
        grid=(n // tm, nout // tn),
        in_specs=x_specs + [
            pl.BlockSpec((1, d), lambda i, j: (0, 0)),
            pl.BlockSpec((d, tn), lambda i, j: (0, j)),
            pl.BlockSpec((1, tn), lambda i, j: (0, j)),
            pl.BlockSpec((tm, LANES), lambda i, j: (i, 0)),
            pl.BlockSpec((tm, LANES), lambda i, j: (i, 0)),
        ],
        out_specs=pl.BlockSpec((tm, tn), lambda i, j: (i, j)),
        out_shape=jax.ShapeDtypeStruct((n, nout), F32),
        scratch_shapes=[pltpu.VMEM((tm, d), BF16)],
        compiler_params=_cparams(("arbitrary", "arbitrary")),
        name="norm_proj",
    )(*x_args, g.reshape(1, d), w, b.reshape(1, nout), cos, sin)


WIN_KEYS_PAD = 256
HALF = LANES // 2


def _sink_attention(q, k, v, sinks_ref, lo, n_keys, n_kv):
    t = q.shape[0]
    sp = WIN_KEYS_PAD
    assert HEAD_DIM_A == HALF and n_kv % 2 == 0 and n_keys < sp
    qb = (q * (HEAD_DIM_A ** -0.5)).astype(BF16)
    low_half = lax.broadcasted_iota(jnp.int32, (sp, LANES), 1) < HALF
    halves = (low_half, jnp.logical_not(low_half))
    ones = tuple(jnp.where(h, 1.0, 0.0).astype(BF16) for h in halves)
    col = lax.broadcasted_iota(jnp.int32, (1, sp), 1)
    visible = jnp.logical_and(col >= lo, col < n_keys)
    is_sink = col == n_keys

    kmat, vmat = {}, {}
    for slab in range(n_kv // 2):
        ks = k[:, slab * LANES:(slab + 1) * LANES]
        vs = v[:, slab * LANES:(slab + 1) * LANES]
        moved = (pltpu.roll(ks, HALF, 1), pltpu.roll(vs, HALF, 1))
        for gp in range(2):
            for par in range(2):
                src_k, src_v = (ks, vs) if par == gp else moved
                kmat[2 * slab + gp, par] = jnp.where(halves[par], src_k, 0.0).astype(BF16)
                vmat[2 * slab + gp, par] = jnp.concatenate(
                    [jnp.where(halves[par], src_v, 0.0).astype(BF16), ones[par]], axis=1)

    ppg = GROUP_A // 2
    outs = []
    for g in range(n_kv):
        qg = jnp.concatenate([qb[:, (g * ppg + i) * LANES:(g * ppg + i + 1) * LANES] for i in range(ppg)], axis=0)
        acc = None
        for par in range(2):
            s = lax.dot_general(qg, kmat[g, par], NT_DIMS, preferred_element_type=F32)
            s = jnp.where(visible, s, NEG_INF)
            s = jnp.concatenate(
                [jnp.where(is_sink, sinks_ref[2 * (g * ppg + i) + par], s[i * t:(i + 1) * t]) for i in range(ppg)],
                axis=0)
            p = jnp.exp(s - jnp.max(s, axis=-1, keepdims=True)).astype(BF16)
            o = jnp.dot(p, vmat[g, par], preferred_element_type=F32)
            acc = o if acc is None else acc + o
        o_norm = acc[:, :LANES] / acc[:, LANES:]
        outs.extend(o_norm[i * t:(i + 1) * t] for i in range(ppg))
    return jnp.concatenate(outs, axis=1)


def _win_prompt_kernel(sinks_ref, q_ref, k0_ref, k1_ref, k2_ref, v0_ref, v1_ref, v2_ref, o_ref, *, n_kv):
    c = pl.program_id(1)
    pad = jnp.zeros((WIN_KEYS_PAD - 3 * CHUNK, k0_ref.shape[1]), F32)
    k = jnp.concatenate([k0_ref[...], k1_ref[...], k2_ref[...], pad], axis=0)
    v = jnp.concatenate([v0_ref[...], v1_ref[...], v2_ref[...], pad], axis=0)
    lo = jnp.maximum(2 - c, 0) * CHUNK
    o_ref[...] = _sink_attention(q_ref[...], k, v, sinks_ref, lo, 3 * CHUNK, n_kv).astype(o_ref.dtype)


def _win_prompt(qkv, sinks, batch, seq, n_kv):
    nq = n_kv * GROUP_A * HEAD_DIM_A
    nk = n_kv * HEAD_DIM_A
    nc = seq // CHUNK
    kcol = nq // nk
    q_spec = pl.BlockSpec((CHUNK, nq), lambda b, c: (b * nc + c, 0))

    def kv_spec(back, col):
        return pl.BlockSpec((CHUNK, nk), lambda b, c: (b * nc + jnp.maximum(c - back, 0), col))

    return pl.pallas_call(
        functools.partial(_win_prompt_kernel, n_kv=n_kv),
        grid=(batch, nc),
        in_specs=[pl.BlockSpec(memory_space=pltpu.SMEM),
                  q_spec, kv_spec(2, kcol), kv_spec(1, kcol), kv_spec(0, kcol),
                  kv_spec(2, kcol + 1), kv_spec(1, kcol + 1), kv_spec(0, kcol + 1)],
        out_specs=pl.BlockSpec((CHUNK, nq), lambda b, c: (b * nc + c, 0)),
        out_shape=jax.ShapeDtypeStruct((batch * seq, nq), BF16),
        compiler_params=_cparams(("arbitrary", "arbitrary")),
        name="win_prompt",
    )(sinks, qkv, qkv, qkv, qkv, qkv, qkv, qkv)


def _win_sample_kernel(sinks_ref, q_ref, kn_ref, vn_ref, ck_ref, cv_ref, o_ref, *, n_kv):
    t = kn_ref.shape[0]
    pad = jnp.zeros((WIN_KEYS_PAD - WINDOW - t, kn_ref.shape[1]), F32)
    k = jnp.concatenate([ck_ref[0], kn_ref[...], pad], axis=0)
    v = jnp.concatenate([cv_ref[0], vn_ref[...], pad], axis=0)
    o_ref[...] = _sink_attention(q_ref[...], k, v, sinks_ref, 0, WINDOW + t, n_kv).astype(o_ref.dtype)


def _win_sample(qkv, cache_k, cache_v, sinks, row0, batch, t, n_kv):
    nq = n_kv * GROUP_A * HEAD_DIM_A
    nk = n_kv * HEAD_DIM_A
    kcol = nq // nk
    blk0 = row0 // t
    return pl.pallas_call(
        functools.partial(_win_sample_kernel, n_kv=n_kv),
        grid=(batch,),
        in_specs=[pl.BlockSpec(memory_space=pltpu.SMEM),
                  pl.BlockSpec((t, nq), lambda b: (blk0 + b, 0)),
                  pl.BlockSpec((t, nk), lambda b: (blk0 + b, kcol)),
                  pl.BlockSpec((t, nk), lambda b: (blk0 + b, kcol + 1)),
                  pl.BlockSpec((1, WINDOW, nk), lambda b: (b, 0, 0)),
                  pl.BlockSpec((1, WINDOW, nk), lambda b: (b, 0, 0))],
        out_specs=pl.BlockSpec((t, nq), lambda b: (b, 0)),
        out_shape=jax.ShapeDtypeStruct((batch * t, nq), BF16),
        compiler_params=_cparams(("arbitrary",)),
        name="win_sample",
    )(sinks, qkv, qkv, qkv, cache_k, cache_v)


SB_STOP_BELOW = -110.0
SB_BLOCK = 256


def _sb_weights(z, mask, carry, u):
    sp = jnp.maximum(z, 0.0) + jnp.log(1.0 + jnp.exp(-jnp.abs(z)))
    ls = -sp if mask is None else jnp.where(mask, -sp, 0.0)
    hi = ls.astype(BF16)
    lo = (ls - hi.astype(F32)).astype(BF16)
    after = (jnp.dot(hi, u, preferred_element_type=F32) + jnp.dot(lo, u, preferred_element_type=F32)) + carry
    a = jnp.exp(z - sp + after)
    if mask is not None:
        a = jnp.where(mask, a, 0.0)
    return a, carry + jnp.sum(ls, axis=-1, keepdims=True)


def _sb_prompt_kernel(q_ref, k_ref, v_ref, u_ref, o_ref, *, seq):
    tb = SB_BLOCK
    scale = HEAD_DIM_B ** -0.5
    u = u_ref[...]
    row = lax.broadcasted_iota(jnp.int32, (tb, tb), 0)
    col = lax.broadcasted_iota(jnp.int32, (tb, tb), 1)
    strictly_earlier = col < row
    no_carry = jnp.zeros((tb, 1), F32)

    def queries(start):
        return (q_ref[pl.ds(start, tb), :] * scale).astype(BF16)

    def scores(q, start, size):
        kb = k_ref[pl.ds(start, size), :].astype(BF16)
        return lax.dot_general(q, kb, NT_DIMS, preferred_element_type=F32)

    def weighted_values(a, start, size):
        vb = v_ref[pl.ds(start, size), :].astype(BF16)
        return jnp.dot(a.astype(BF16), vb, preferred_element_type=F32)

    a, _ = _sb_weights(scores(queries(0), 0, tb), strictly_earlier, no_carry, u)
    o_ref[pl.ds(0, tb), :] = weighted_values(a, 0, tb).astype(o_ref.dtype)

    def q_block(qi, _):
        q0 = pl.multiple_of(qi * tb, tb)
        p0 = pl.multiple_of(q0 - tb, tb)
        q = queries(q0)
        z = scores(q, p0, 2 * tb)
        a_own, carry = _sb_weights(z[:, tb:], strictly_earlier, no_carry, u)
        a_prev, carry = _sb_weights(z[:, :tb], None, carry, u)
        acc = weighted_values(jnp.concatenate([a_prev, a_own], axis=1), p0, 2 * tb)

        def more(state):
            kj, cmax, _, _ = state
            return jnp.logical_and(kj >= 0, cmax > SB_STOP_BELOW)

        def earlier_block(state):
            kj, _, carry, acc = state
            k0 = pl.multiple_of(kj * tb, tb)
            a, carry = _sb_weights(scores(q, k0, tb), None, carry, u)
            return kj - 1, jnp.max(carry), carry, acc + weighted_values(a, k0, tb)

        _, _, _, acc = lax.while_loop(more, earlier_block, (qi - 2, jnp.max(carry), carry, acc))
        o_ref[pl.ds(q0, tb), :] = acc.astype(o_ref.dtype)
        return 0

    lax.fori_loop(1, seq // tb, q_block, 0)


def _sb_prompt(qkv, u, batch, seq, n_heads):
    hd = HEAD_DIM_B
    assert seq % SB_BLOCK == 0
    return pl.pallas_call(
        functools.partial(_sb_prompt_kernel, seq=seq),
        grid=(batch, n_heads),
        in_specs=[pl.BlockSpec((seq, hd), lambda b, h: (b, h)),
                  pl.BlockSpec((seq, hd), lambda b, h: (b, n_heads + h)),
                  pl.BlockSpec((seq, hd), lambda b, h: (b, 2 * n_heads + h)),
                  pl.BlockSpec((SB_BLOCK, SB_BLOCK), lambda b, h: (0, 0))],
        out_specs=pl.BlockSpec((seq, hd), lambda b, h: (b, h)),
        out_shape=jax.ShapeDtypeStruct((batch * seq, n_heads * hd), BF16),
        compiler_params=_cparams(("arbitrary", "arbitrary")),
        name="sb_prompt",
    )(qkv, qkv, qkv, u)


SB_NEW_PAD = SB_BLOCK


def _sb_sample_kernel(q_ref, kn_ref, vn_ref, u_ref, ck_ref, cv_ref, o_ref, kbuf, vbuf, sem,
                      *, t, n_heads, n_blocks):
    b = pl.program_id(0)
    tb = SB_BLOCK
    hd = HEAD_DIM_B
    d = n_heads * hd
    rows = n_heads * t
    scale = hd ** -0.5
    u = u_ref[...]

    def block_copies(j, slot):
        start = pl.multiple_of((n_blocks - 1 - j) * tb, tb)
        out = []
        for h in range(n_heads):
            out.append(pltpu.make_async_copy(ck_ref.at[0, b, pl.ds(start, tb), h, :], kbuf.at[slot, h],
                                             sem.at[slot, 0]))
            out.append(pltpu.make_async_copy(cv_ref.at[0, b, pl.ds(start, tb), h, :], vbuf.at[slot, h],
                                             sem.at[slot, 1]))
        return out

    for c in block_copies(0, 0):
        c.start()

    q = q_ref[...] * scale
    qh = [q[:, h * hd:(h + 1) * hd].astype(BF16) for h in range(n_heads)]

    pad = jnp.zeros((SB_NEW_PAD - t, d), F32)
    kn = jnp.concatenate([kn_ref[...], pad], axis=0).astype(BF16)
    vn = jnp.concatenate([vn_ref[...], pad], axis=0).astype(BF16)
    z = jnp.concatenate(
        [lax.dot_general(qh[h], kn[:, h * hd:(h + 1) * hd], NT_DIMS, preferred_element_type=F32)
         for h in range(n_heads)], axis=0)
    rr = lax.broadcasted_iota(jnp.int32, (rows, SB_NEW_PAD), 0) % t
    cc = lax.broadcasted_iota(jnp.int32, (rows, SB_NEW_PAD), 1)
    a, carry = _sb_weights(z, cc < rr, jnp.zeros((rows, 1), F32), u[:SB_NEW_PAD, :SB_NEW_PAD])
    a = a.astype(BF16)
    acc = jnp.concatenate(
        [jnp.dot(a[h * t:(h + 1) * t], vn[:, h * hd:(h + 1) * hd], preferred_element_type=F32)
         for h in range(n_heads)], axis=1)

    def more(state):
        j, cmax, _, _ = state
        return jnp.logical_and(j < n_blocks, cmax > SB_STOP_BELOW)

    def cache_block(state):
        j, _, carry, acc = state
        slot = j % 2
        for c in block_copies(j, slot):
            c.wait()

        @pl.when(j + 1 < n_blocks)
        def _():
            for c in block_copies(j + 1, 1 - slot):
                c.start()

        z = jnp.concatenate(
            [lax.dot_general(qh[h], kbuf[slot, h].astype(BF16), NT_DIMS, preferred_element_type=F32)
             for h in range(n_heads)], axis=0)
        a, carry = _sb_weights(z, None, carry, u)
        a = a.astype(BF16)
        out = jnp.concatenate(
            [jnp.dot(a[h * t:(h + 1) * t], vbuf[slot, h].astype(BF16), preferred_element_type=F32)
             for h in range(n_heads)], axis=1)
        return j + 1, jnp.max(carry), carry, acc + out

    j, _, _, acc = lax.while_loop(more, cache_block, (jnp.int32(0), jnp.max(carry), carry, acc))
    o_ref[...] = acc.astype(o_ref.dtype)

    @pl.when(j < n_blocks)
    def _():
        for c in block_copies(j, j % 2):
            c.wait()


def _sb_sample(qkv, cache_k, cache_v, u, row0, batch, t, n_heads):
    hd = HEAD_DIM_B
    d = n_heads * hd
    past = cache_k.shape[2]
    assert past % SB_BLOCK == 0 and t <= SB_NEW_PAD
    blk0 = row0 // t
    return pl.pallas_call(
        functools.partial(_sb_sample_kernel, t=t, n_heads=n_heads, n_blocks=past // SB_BLOCK),
        grid=(batch,),
        in_specs=[pl.BlockSpec((t, d), lambda b: (blk0 + b, 0)),
                  pl.BlockSpec((t, d), lambda b: (blk0 + b, 1)),
                  pl.BlockSpec((t, d), lambda b: (blk0 + b, 2)),
                  pl.BlockSpec((SB_BLOCK, SB_BLOCK), lambda b: (0, 0)),
                  pl.BlockSpec(memory_space=pl.ANY),
                  pl.BlockSpec(memory_space=pl.ANY)],
        out_specs=pl.BlockSpec((t, d), lambda b: (b, 0)),
        out_shape=jax.ShapeDtypeStruct((batch * t, d), BF16),
        scratch_shapes=[pltpu.VMEM((2, n_heads, SB_BLOCK, hd), F32),
                        pltpu.VMEM((2, n_heads, SB_BLOCK, hd), F32),
                        pltpu.SemaphoreType.DMA((2, 2))],
        compiler_params=_cparams(("arbitrary",)),
        name="sb_sample",
    )(qkv, qkv, qkv, u, cache_k, cache_v)


def _pack_bf16_pair(a, b):
    ab = lax.bitcast_convert_type(a.astype(BF16).astype(F32), jnp.uint32)
    bb = lax.bitcast_convert_type(b.astype(BF16).astype(F32), jnp.uint32)
    return (ab >> 16) | (bb & jnp.uint32(0xFFFF0000))


def _unpack_bf16_pair(p):
    a = lax.bitcast_convert_type(p << 16, F32)
    b = lax.bitcast_convert_type(p & jnp.uint32(0xFFFF0000), F32)
    return a.astype(BF16), b.astype(BF16)


def _oproj_router_kernel(op_ref, os_ref, *refs, prompt_tiles, split_x):
    if split_x:
        x_ref, xs_ref, wo_ref, g_ref, wr_ref, br_ref, x1_ref, tp_ref, route_ref = refs
    else:
        x_ref, wo_ref, g_ref, wr_ref, br_ref, x1_ref, tp_ref, route_ref = refs
    in_prompt = pl.program_id(0) < prompt_tiles
    o = jnp.where(in_prompt, op_ref[...], os_ref[...])
    x = x_ref[...]
    if split_x:
        x = jnp.where(in_prompt, x, xs_ref[...])
    x1 = x + jnp.dot(o, wo_ref[...], preferred_element_type=F32)
    x1_ref[...] = x1
    ms = jnp.mean(x1 * x1, axis=-1, keepdims=True)
    t = x1 * lax.rsqrt(ms + RMS_EPS) * g_ref[...]
    half = t.shape[1] // 2
    tp_ref[...] = _pack_bf16_pair(t[:, :half], t[:, half:])
    hi = t.astype(BF16)
    lo = (t - hi.astype(F32)).astype(BF16)
    r_hi = jnp.dot(hi, wr_ref[...], preferred_element_type=F32)
    r_lo = jnp.dot(lo, wr_ref[...], preferred_element_type=F32)
    logits = r_hi + pltpu.roll(r_hi, LANES - ROUTER_LO_LANE, 1) + r_lo + br_ref[...]

    lane = lax.broadcasted_iota(jnp.int32, logits.shape, 1)
    lane_f = lane.astype(F32)
    big = float(LANES)
    is_group = lane < N_GROUPS
    gl = jnp.where(is_group, logits, -jnp.inf)
    gmax = jnp.max(gl, axis=-1, keepdims=True)
    gidx = jnp.min(jnp.where(gl == gmax, lane_f, big), axis=-1, keepdims=True)
    gsum = jnp.sum(jnp.where(is_group, jnp.exp(logits - gmax), 0.0), axis=-1, keepdims=True)
    g_w = 1.0 / gsum
    expert = lane - GATE_LANE0
    in_group = (expert >= 0) & (expert < N_EXPERTS) & ((expert // EXPERTS_PER_GROUP).astype(F32) == gidx)
    el = jnp.where(in_group, logits, -jnp.inf)
    v1 = jnp.max(el, axis=-1, keepdims=True)
    i1 = jnp.min(jnp.where(el == v1, lane_f, big), axis=-1, keepdims=True)
    el2 = jnp.where(lane_f == i1, -jnp.inf, el)
    v2 = jnp.max(el2, axis=-1, keepdims=True)
    i2 = jnp.min(jnp.where(el2 == v2, lane_f, big), axis=-1, keepdims=True)
    e21 = jnp.exp(v2 - v1)
    w1 = g_w / (1.0 + e21)
    w2 = g_w * e21 / (1.0 + e21)
    route_ref[...] = jnp.where(lane == 0, i1 - GATE_LANE0,
                               jnp.where(lane == 1, i2 - GATE_LANE0,
                                         jnp.where(lane == 2, w1, jnp.where(lane == 3, w2, 0.0))))


def _oproj_router(o_p, o_s, x, wo, g, wr, br):
    d = wo.shape[0]
    o_args, o_specs, tm, n, prompt_tiles = _row_specs((o_p, o_s), 256)
    if isinstance(x, tuple):
        x_args, x_specs = list(x), list(o_specs)
    else:
        x_args, x_specs = [x], [pl.BlockSpec((tm, d), lambda i: (i, 0))]
    row = lambda i: (i, 0)
    const = lambda i: (0, 0)
    return pl.pallas_call(
        functools.partial(_oproj_router_kernel, prompt_tiles=prompt_tiles, split_x=isinstance(x, tuple)),
        grid=(n // tm,),
        in_specs=o_specs + x_specs + [
                  pl.BlockSpec((d, d), const), pl.BlockSpec((1, d), const),
                  pl.BlockSpec((d, LANES), const), pl.BlockSpec((1, LANES), const)],
        out_specs=[pl.BlockSpec((tm, d), row), pl.BlockSpec((tm, d // 2), row), pl.BlockSpec((tm, LANES), row)],
        out_shape=[jax.ShapeDtypeStruct((n, d), F32), jax.ShapeDtypeStruct((n, d // 2), jnp.uint32),
                   jax.ShapeDtypeStruct((n, LANES), F32)],
        compiler_params=_cparams(("arbitrary",)),
        name="oproj_router",
    )(*o_args, *x_args, wo, g.reshape(1, d), wr, br)


MOE_TILE = 256
TOP_K = 2


def _moe_plan(route, n_tiles):
    n = route.shape[0]
    e_flat = route[:, :TOP_K].astype(jnp.int32).reshape(n * TOP_K)
    onehot = (e_flat[:, None] == jnp.arange(N_EXPERTS, dtype=jnp.int32)[None, :]).astype(jnp.int32)
    csum = jnp.cumsum(onehot, axis=0)
    rank = jnp.sum(csum * onehot, axis=1) - 1
    counts = csum[-1]
    tiles = (counts + MOE_TILE - 1) // MOE_TILE
    tile_end = jnp.cumsum(tiles)
    row_start = (tile_end - tiles) * MOE_TILE
    pos = jnp.sum(onehot * row_start[None, :], axis=1) + rank
    n_used = tile_end[-1]
    tile_id = jnp.minimum(jnp.arange(n_tiles, dtype=jnp.int32), n_used - 1)
    tile_expert = jnp.sum((tile_end[None, :] <= tile_id[:, None]).astype(jnp.int32), axis=1)
    order = jnp.argsort(e_flat, stable=True).astype(jnp.int32)
    row = jnp.arange(n_tiles * MOE_TILE, dtype=jnp.int32)
    row_expert = jnp.repeat(tile_expert, MOE_TILE)
    in_expert = row - row_start[row_expert]
    first_copy = jnp.cumsum(counts) - counts
    valid = in_expert < counts[row_expert]
    copy = jnp.where(valid, first_copy[row_expert] + in_expert, 0)
    row_token = jnp.where(valid, order[copy] // TOP_K, 0)
    return pos.astype(jnp.int32), row_token.astype(jnp.int32), tile_expert.astype(jnp.int32)


EXPERT_COLS = 256


def _experts_kernel(te_ref, rt_ref, tp_ref, wg_ref, wu_ref, wd_ref, ys_ref, xbuf, wg_s, wu_s, wd_s, sem):
    i = pl.program_id(0)
    n_steps = pl.num_programs(0)
    slot = i % 2

    def row_copy(tile, to_slot, r):
        tok = rt_ref[tile * MOE_TILE + r]
        return pltpu.make_async_copy(tp_ref.at[pl.ds(tok, 1)], xbuf.at[to_slot, pl.ds(r, 1)], sem.at[to_slot])

    def tile_wait(in_slot):
        pltpu.make_async_copy(tp_ref.at[pl.ds(0, MOE_TILE)], xbuf.at[in_slot], sem.at[in_slot]).wait()

    @pl.when(i == 0)
    def _():
        def body(r, carry):
            row_copy(0, 0, r).start()
            return carry
        lax.fori_loop(0, MOE_TILE, body, 0)

    @pl.when(jnp.logical_or(i == 0, te_ref[i] != te_ref[jnp.maximum(i - 1, 0)]))
    def _():
        wg_s[...] = wg_ref[0, 0].astype(BF16)
        wu_s[...] = wu_ref[0, 0].astype(BF16)
        wd_s[...] = wd_ref[0, 0].astype(BF16)

    tile_wait(slot)
    nxt = jnp.minimum(i + 1, n_steps - 1)
    a, b = _unpack_bf16_pair(xbuf[slot])
    x = jnp.concatenate([a, b], axis=1)
    de = wg_s.shape[1]
    d = wd_s.shape[1]
    wc = min(EXPERT_COLS, de)
    pieces = 2 * (de // wc) + d // wc
    per_piece = MOE_TILE // pieces
    issued = [0]

    def issue_some(last=False):
        stop = MOE_TILE if last else issued[0] + per_piece
        for r in range(issued[0], stop):
            row_copy(nxt, 1 - slot, r).start()
        issued[0] = stop

    def cols(w_ref, c):
        return w_ref[:, c * wc:(c + 1) * wc]

    hid = []
    for c in range(de // wc):
        issue_some()
        hg = jnp.dot(x, cols(wg_s, c), preferred_element_type=F32)
        issue_some()
        hu = jnp.dot(x, cols(wu_s, c), preferred_element_type=F32)
        hid.append((hg * jax.nn.sigmoid(hg) * hu).astype(BF16))
    hid = jnp.concatenate(hid, axis=1)
    for c in range(d // wc):
        issue_some(last=c == d // wc - 1)
        ys_ref[:, c * wc:(c + 1) * wc] = jnp.dot(hid, cols(wd_s, c), preferred_element_type=F32)

    @pl.when(i == n_steps - 1)
    def _():
        tile_wait(1 - slot)


def _experts(tile_expert, row_token, tp, wg, wu, wd, layer):
    n_tiles = tile_expert.shape[0]
    dw = tp.shape[1]
    _, _, d, de = wg.shape
    grid_spec = pltpu.PrefetchScalarGridSpec(
        num_scalar_prefetch=2, grid=(n_tiles,),
        in_specs=[pl.BlockSpec(memory_space=pl.ANY),
                  pl.BlockSpec((1, 1, d, de), lambda i, te, rt: (layer, te[i], 0, 0)),
                  pl.BlockSpec((1, 1, d, de), lambda i, te, rt: (layer, te[i], 0, 0)),
                  pl.BlockSpec((1, 1, de, d), lambda i, te, rt: (layer, te[i], 0, 0))],
        out_specs=pl.BlockSpec((MOE_TILE, d), lambda i, te, rt: (i, 0)),
        scratch_shapes=[pltpu.VMEM((2, MOE_TILE, dw), jnp.uint32),
                        pltpu.VMEM((d, de), BF16), pltpu.VMEM((d, de), BF16), pltpu.VMEM((de, d), BF16),
                        pltpu.SemaphoreType.DMA((2,))])
    return pl.pallas_call(
        _experts_kernel,
        grid_spec=grid_spec,
        out_shape=jax.ShapeDtypeStruct((n_tiles * MOE_TILE, d), F32),
        compiler_params=_cparams(("arbitrary",)),
        name="moe_experts",
    )(tile_expert, row_token, tp, wg, wu, wd)


COMBINE_ROWS = 32


def _combine_kernel(pos_ref, ys_ref, x1_ref, route_ref, *rest, tc, prompt_tiles):
    if prompt_tiles is None:
        o_ref, buf_ref, sem = rest
    else:
        g_ref, yp_ref, ysm_ref, buf_ref, sem = rest
    i = pl.program_id(0)
    n_steps = pl.num_programs(0)
    slot = i % 2

    def issue_rows(step, to_slot, row0, n_rows):
        for r in range(n_rows):
            for s in range(TOP_K):
                p = pos_ref[TOP_K * (step * tc + row0 + r) + s]
                pltpu.make_async_copy(ys_ref.at[pl.ds(p, 1)], buf_ref.at[to_slot, s, pl.ds(row0 + r, 1)],
                                      sem.at[to_slot]).start()

    @pl.when(i == 0)
    def _():
        lax.fori_loop(0, tc // COMBINE_ROWS,
                      lambda c, carry: (issue_rows(0, 0, c * COMBINE_ROWS, COMBINE_ROWS), carry)[1], 0)

    for s in range(TOP_K):
        pltpu.make_async_copy(ys_ref.at[pl.ds(0, tc)], buf_ref.at[slot, s], sem.at[slot]).wait()

    def run(prefetch, out_ref):
        def body(c, carry):
            r0 = pl.multiple_of(c * COMBINE_ROWS, COMBINE_ROWS)
            if prefetch:
                issue_rows(i + 1, 1 - slot, r0, COMBINE_ROWS)
            rows = pl.ds(r0, COMBINE_ROWS)
            route = route_ref[rows, :]
            x = (x1_ref[rows, :] + buf_ref[slot, 0, rows, :] * route[:, TOP_K:TOP_K + 1]
                 + buf_ref[slot, 1, rows, :] * route[:, TOP_K + 1:TOP_K + 2])
            if prompt_tiles is not None:
                ms = jnp.mean(x * x, axis=-1, keepdims=True)
                x = x * lax.rsqrt(ms + RMS_EPS) * g_ref[...]
            out_ref[rows, :] = x
            return carry
        lax.fori_loop(0, tc // COMBINE_ROWS, body, 0)

    more = i + 1 < n_steps
    for prefetch, cond in ((True, more), (False, jnp.logical_not(more))):
        if prompt_tiles is None:
            pl.when(cond)(functools.partial(run, prefetch, o_ref))
        else:
            pl.when(jnp.logical_and(cond, i < prompt_tiles))(functools.partial(run, prefetch, yp_ref))
            pl.when(jnp.logical_and(cond, i >= prompt_tiles))(functools.partial(run, prefetch, ysm_ref))


def _combine(pos, ys, x1, route, n_p, final_gain=None):
    n, d = x1.shape
    tc, prompt_tiles = _split_tiles(n_p, n - n_p, 256)
    assert tc % COMBINE_ROWS == 0
    row = lambda i, pos: (i, 0)
    in_specs = [pl.BlockSpec(memory_space=pl.ANY), pl.BlockSpec((tc, d), row), pl.BlockSpec((tc, LANES), row)]
    args = [pos, ys, x1, route]
    if final_gain is None:
        out_specs = pl.BlockSpec((tc, d), row)
        out_shape = jax.ShapeDtypeStruct((n, d), F32)
    else:
        in_specs.append(pl.BlockSpec((1, d), lambda i, pos: (0, 0)))
        args.append(final_gain.reshape(1, d))
        out_specs = [pl.BlockSpec((tc, d), lambda i, pos: (jnp.minimum(i, prompt_tiles - 1), 0)),
                     pl.BlockSpec((tc, d), lambda i, pos: (jnp.maximum(i - prompt_tiles, 0), 0))]
        out_shape = [jax.ShapeDtypeStruct((n_p, d), F32), jax.ShapeDtypeStruct((n - n_p, d), F32)]
    grid_spec = pltpu.PrefetchScalarGridSpec(
        num_scalar_prefetch=1, grid=(n // tc,), in_specs=in_specs, out_specs=out_specs,
        scratch_shapes=[pltpu.VMEM((2, TOP_K, tc, d), F32), pltpu.SemaphoreType.DMA((2,))])
    return pl.pallas_call(
        functools.partial(_combine_kernel, tc=tc, prompt_tiles=None if final_gain is None else prompt_tiles),
        grid_spec=grid_spec,
        out_shape=out_shape,
        compiler_params=_cparams(("arbitrary",)),
        name="moe_combine",
    )(*args)


def _moe(tp, route, x1, wg, wu, wd, layer, n_p, final_gain=None):
    n = x1.shape[0]
    n_tiles = -(-n * TOP_K // MOE_TILE) + N_EXPERTS
    pos, row_token, tile_expert = _moe_plan(route, n_tiles)
    ys = _experts(tile_expert, row_token, tp, wg, wu, wd, layer)
    return _combine(pos, ys, x1, route, n_p, final_gain)


def _rope_tables(pos):
    half = HEAD_DIM_A // 2
    inv = ROPE_THETA ** (-jnp.arange(half, dtype=F32) / half)
    ang = pos.astype(F32)[:, None] * inv[None, :]
    cos = jnp.cos(ang)
    sin = jnp.sin(ang)
    reps = LANES // HEAD_DIM_A
    return jnp.tile(jnp.concatenate([cos, cos], axis=1), (1, reps)), \
        jnp.tile(jnp.concatenate([-sin, sin], axis=1), (1, reps))


def _router_weights(w_group, b_group, w_router, b_router):
    d = w_group.shape[0]
    w = jnp.concatenate([w_group, w_router], axis=1).astype(F32)
    n_log = w.shape[1]
    hi = w.astype(BF16)
    lo = (w - hi.astype(F32)).astype(BF16)
    wr = jnp.zeros((d, LANES), BF16)
    wr = wr.at[:, :n_log].set(hi).at[:, ROUTER_LO_LANE:ROUTER_LO_LANE + n_log].set(lo)
    br = jnp.zeros((1, LANES), F32).at[0, :n_log].set(jnp.concatenate([b_group, b_router]).astype(F32))
    return wr, br


def kernel(x_prompt, x_sample, cache_win_k, cache_win_v, cache_sb_k, cache_sb_v, norm_mix, norm_ffn, norm_final,
           a_w_qkv, a_b_qkv, a_sinks, a_w_o, b_w_qkv, b_w_o, moe_w_group, moe_b_group, moe_w_router,
           moe_b_router, moe_w_gate, moe_w_up, moe_w_down):
    bp, sp, d = x_prompt.shape
    bs, ts, _ = x_sample.shape
    n_p = bp * sp
    n_s = bs * ts
    past = cache_sb_k.shape[2]
    n_kv = cache_win_k.shape[3]
    n_heads_b = cache_sb_k.shape[3]
    nq_a = n_kv * GROUP_A * HEAD_DIM_A
    nk_a = n_kv * HEAD_DIM_A
    assert sp % CHUNK == 0 and past % CHUNK == 0 and ts <= CHUNK and n_p % ts == 0
    assert cache_win_k.shape[2] == WINDOW and d == n_heads_b * HEAD_DIM_B == nq_a

    x = (x_prompt.reshape(n_p, d), x_sample.reshape(n_s, d))
    pos = jnp.concatenate([jnp.tile(jnp.arange(sp, dtype=jnp.int32), bp),
                           jnp.tile(past + jnp.arange(ts, dtype=jnp.int32), bs)])
    cos, sin = _rope_tables(pos)
    u = (lax.broadcasted_iota(jnp.int32, (SB_BLOCK, SB_BLOCK), 0)
         > lax.broadcasted_iota(jnp.int32, (SB_BLOCK, SB_BLOCK), 1)).astype(BF16)

    outs = {}
    for i in range(2):
        if i == 0:
            tn = _pick(a_w_qkv.shape[2], 1280)
            qkv = _norm_proj(x, norm_mix[i], a_w_qkv[0].astype(BF16), a_b_qkv[0], cos, sin, nq_a + nk_a, tn)
            sinks = a_sinks[0].astype(F32)
            o_p = _win_prompt(qkv, sinks, bp, sp, n_kv)
            o_s = _win_sample(qkv, cache_win_k[0].reshape(bs, WINDOW, nk_a), cache_win_v[0].reshape(bs, WINDOW, nk_a),
                              sinks, n_p, bs, ts, n_kv)
            k_all = qkv[:, nq_a:nq_a + nk_a]
            v_all = qkv[:, nq_a + nk_a:]
            outs["wkp"] = k_all[:n_p].reshape(bp, sp, n_kv, HEAD_DIM_A)[:, sp - WINDOW:][None]
            outs["wvp"] = v_all[:n_p].reshape(bp, sp, n_kv, HEAD_DIM_A)[:, sp - WINDOW:][None]
            outs["wks"] = jnp.concatenate(
                [cache_win_k[0], k_all[n_p:].reshape(bs, ts, n_kv, HEAD_DIM_A)], axis=1)[:, -WINDOW:][None]
            outs["wvs"] = jnp.concatenate(
                [cache_win_v[0], v_all[n_p:].reshape(bs, ts, n_kv, HEAD_DIM_A)], axis=1)[:, -WINDOW:][None]
            w_o = a_w_o[0]
        else:
            zeros_b = jnp.zeros((b_w_qkv.shape[2],), F32)
            qkv = _norm_proj(x, norm_mix[i], b_w_qkv[0].astype(BF16), zeros_b, cos, sin, 0, d)
            o_p = _sb_prompt(qkv, u, bp, sp, n_heads_b)
            o_s = _sb_sample(qkv, cache_sb_k, cache_sb_v, u, n_p, bs, ts, n_heads_b)
            k_all = qkv[:, d:2 * d]
            v_all = qkv[:, 2 * d:]
            outs["skp"] = k_all[:n_p].reshape(1, bp, sp, n_heads_b, HEAD_DIM_B)
            outs["svp"] = v_all[:n_p].reshape(1, bp, sp, n_heads_b, HEAD_DIM_B)
            outs["sks"] = k_all[n_p:].reshape(1, bs, ts, n_heads_b, HEAD_DIM_B)
            outs["svs"] = v_all[n_p:].reshape(1, bs, ts, n_heads_b, HEAD_DIM_B)
            w_o = b_w_o[0]
        wr, br = _router_weights(moe_w_group[i], moe_b_group[i], moe_w_router[i], moe_b_router[i])
        x1, tp, route = _oproj_router(o_p, o_s, x, w_o.astype(BF16), norm_ffn[i], wr, br)
        x = _moe(tp, route, x1, moe_w_gate, moe_w_up, moe_w_down, i, n_p,
                 final_gain=norm_final if i == 1 else None)
    y_p, y_s = x
    return (y_p.reshape(bp, sp, d), y_s.reshape(bs, ts, d),
            outs["wkp"], outs["wvp"], outs["wks"], outs["wvs"],
            outs["skp"], outs["svp"], outs["sks"], outs["svs"])
```

```python
import functools
import math

import jax
import jax.numpy as jnp
from jax import lax
from jax.experimental import pallas as pl
from jax.experimental.pallas import tpu as pltpu

F32 = jnp.float32
BF16 = jnp.bfloat16

CHUNK = 64
WINDOW = 128
HEAD_DIM_A = 64
GROUP_A = 8
HEAD_DIM_B = 128
N_GROUPS = 4
EXPERTS_PER_GROUP = 8
N_EXPERTS = N_GROUPS * EXPERTS_PER_GROUP
ROPE_THETA = 10000.0
RMS_EPS = 1e-6
NEG_INF = -1e30

LANES = 128
ROUTER_LO_LANE = 64
GATE_LANE0 = N_GROUPS
VMEM_LIMIT = 56 * 1024 * 1024

NT_DIMS = (((1,), (1,)), ((), ()))


def _cparams(sem):
    return pltpu.CompilerParams(dimension_semantics=sem, vmem_limit_bytes=VMEM_LIMIT)


def _pick(n, pref, align=8):
    for t in range(min(n, pref), 0, -1):
        if n % t == 0 and t % align == 0:
            return t
    raise ValueError((n, pref, align))


def _norm_proj_kernel(*refs, rope_cols, tn, prompt_tiles):
    if prompt_tiles is None:
        x_ref, g_ref, w_ref, b_ref, cos_ref, sin_ref, o_ref, h_ref = refs
    else:
        x_ref, xs_ref, g_ref, w_ref, b_ref, cos_ref, sin_ref, o_ref, h_ref = refs
    j = pl.program_id(1)

    @pl.when(j == 0)
    def _():
        x = x_ref[...]
        if prompt_tiles is not None:
            x = jnp.where(pl.program_id(0) < prompt_tiles, x, xs_ref[...])
        ms = jnp.mean(x * x, axis=-1, keepdims=True)
        h_ref[...] = (x * lax.rsqrt(ms + RMS_EPS) * g_ref[...]).astype(BF16)

    y = jnp.dot(h_ref[...], w_ref[...], preferred_element_type=F32) + b_ref[...]
    if rope_cols:
        reps = tn // LANES
        cos = jnp.concatenate([cos_ref[...]] * reps, axis=1)
        sin = jnp.concatenate([sin_ref[...]] * reps, axis=1)
        lane = lax.broadcasted_iota(jnp.int32, y.shape, 1)
        half = HEAD_DIM_A // 2
        first = (lane % HEAD_DIM_A) < half
        swapped = jnp.where(first, pltpu.roll(y, tn - half, 1), pltpu.roll(y, half, 1))
        roped = y * cos + swapped * sin
        y = jnp.where(lane + j * tn < rope_cols, roped, y)
    o_ref[...] = y


def _split_tiles(n_p, n_s, pref):
    tm = _pick(math.gcd(n_p, n_s), pref, 16)
    return tm, n_p // tm


def _row_specs(x, pref):
    if not isinstance(x, tuple):
        n, d = x.shape
        tm = _pick(n, pref)
        return [x], [pl.BlockSpec((tm, d), lambda i, *_: (i, 0))], tm, n, None
    x_p, x_s = x
    d = x_p.shape[1]
    tm, prompt_tiles = _split_tiles(x_p.shape[0], x_s.shape[0], pref)
    specs = [pl.BlockSpec((tm, d), lambda i, *_: (jnp.minimum(i, prompt_tiles - 1), 0)),
             pl.BlockSpec((tm, d), lambda i, *_: (jnp.maximum(i - prompt_tiles, 0), 0))]
    return [x_p, x_s], specs, tm, x_p.shape[0] + x_s.shape[0], prompt_tiles


def _norm_proj(x, g, w, b, cos, sin, rope_cols, tn):
    x_args, x_specs, tm, n, prompt_tiles = _row_specs(x, 512)
    d, nout = w.shape
    kern = functools.partial(_norm_proj_kernel, rope_cols=rope_cols, tn=tn, prompt_tiles=prompt_tiles)
    return pl.pallas_call(
        kern,
        grid=(n // tm, nout // tn),
        in_specs=x_specs + [
            pl.BlockSpec((1, d), lambda i, j: (0, 0)),
            pl.BlockSpec((d, tn), lambda i, j: (0, j)),
            pl.BlockSpec((1, tn), lambda i, j: (0, j)),
            pl.BlockSpec((tm, LANES), lambda i, j: (i, 0)),
            pl.BlockSpec((tm, LANES), lambda i, j: (i, 0)),
        ],
        out_specs=pl.BlockSpec((tm, tn), lambda i, j: (i, j)),
        out_shape=jax.ShapeDtypeStruct((n, nout), F32),
        scratch_shapes=[pltpu.VMEM((tm, d), BF16)],
        compiler_params=_cparams(("arbitrary", "arbitrary")),
        name="norm_proj",
    )(*x_args, g.reshape(1, d), w, b.reshape(1, nout), cos, sin)


WIN_KEYS_PAD = 256
HALF = LANES // 2


def _sink_attention(q, k, v, sinks_ref, lo, n_keys, n_kv):
    t = q.shape[0]
    sp = WIN_KEYS_PAD
    assert HEAD_DIM_A == HALF and n_kv % 2 == 0 and n_keys < sp
    qb = (q * (HEAD_DIM_A ** -0.5)).astype(BF16)
    low_half = lax.broadcasted_iota(jnp.int32, (sp, LANES), 1) < HALF
    halves = (low_half, jnp.logical_not(low_half))
    ones = tuple(jnp.where(h, 1.0, 0.0).astype(BF16) for h in halves)
    col = lax.broadcasted_iota(jnp.int32, (1, sp), 1)
    visible = jnp.logical_and(col >= lo, col < n_keys)
    is_sink = col == n_keys

    kmat, vmat = {}, {}
    for slab in range(n_kv // 2):
        ks = k[:, slab * LANES:(slab + 1) * LANES]
        vs = v[:, slab * LANES:(slab + 1) * LANES]
        moved = (pltpu.roll(ks, HALF, 1), pltpu.roll(vs, HALF, 1))
        for gp in range(2):
            for par in range(2):
                src_k, src_v = (ks, vs) if par == gp else moved
                kmat[2 * slab + gp, par] = jnp.where(halves[par], src_k, 0.0).astype(BF16)
                vmat[2 * slab + gp, par] = jnp.concatenate(
                    [jnp.where(halves[par], src_v, 0.0).astype(BF16), ones[par]], axis=1)

    ppg = GROUP_A // 2
    outs = []
    for g in range(n_kv):
        qg = jnp.concatenate([qb[:, (g * ppg + i) * LANES:(g * ppg + i + 1) * LANES] for i in range(ppg)], axis=0)
        acc = None
        for par in range(2):
            s = lax.dot_general(qg, kmat[g, par], NT_DIMS, preferred_element_type=F32)
            s = jnp.where(visible, s, NEG_INF)
            s = jnp.concatenate(
                [jnp.where(is_sink, sinks_ref[2 * (g * ppg + i) + par], s[i * t:(i + 1) * t]) for i in range(ppg)],
                axis=0)
            p = jnp.exp(s - jnp.max(s, axis=-1, keepdims=True)).astype(BF16)
            o = jnp.dot(p, vmat[g, par], preferred_element_type=F32)
            acc = o if acc is None else acc + o
        o_norm = acc[:, :LANES] / acc[:, LANES:]
        outs.extend(o_norm[i * t:(i + 1) * t] for i in range(ppg))
    return jnp.concatenate(outs, axis=1)


def _win_prompt_kernel(sinks_ref, q_ref, k0_ref, k1_ref, k2_ref, v0_ref, v1_ref, v2_ref, o_ref, *, n_kv):
    c = pl.program_id(1)
    pad = jnp.zeros((WIN_KEYS_PAD - 3 * CHUNK, k0_ref.shape[1]), F32)
    k = jnp.concatenate([k0_ref[...], k1_ref[...], k2_ref[...], pad], axis=0)
    v = jnp.concatenate([v0_ref[...], v1_ref[...], v2_ref[...], pad], axis=0)
    lo = jnp.maximum(2 - c, 0) * CHUNK
    o_ref[...] = _sink_attention(q_ref[...], k, v, sinks_ref, lo, 3 * CHUNK, n_kv).astype(o_ref.dtype)


def _win_prompt(qkv, sinks, batch, seq, n_kv):
    nq = n_kv * GROUP_A * HEAD_DIM_A
    nk = n_kv * HEAD_DIM_A
    nc = seq // CHUNK
    kcol = nq // nk
    q_spec = pl.BlockSpec((CHUNK, nq), lambda b, c: (b * nc + c, 0))

    def kv_spec(back, col):
        return pl.BlockSpec((CHUNK, nk), lambda b, c: (b * nc + jnp.maximum(c - back, 0), col))

    return pl.pallas_call(
        functools.partial(_win_prompt_kernel, n_kv=n_kv),
        grid=(batch, nc),
        in_specs=[pl.BlockSpec(memory_space=pltpu.SMEM),
                  q_spec, kv_spec(2, kcol), kv_spec(1, kcol), kv_spec(0, kcol),
                  kv_spec(2, kcol + 1), kv_spec(1, kcol + 1), kv_spec(0, kcol + 1)],
        out_specs=pl.BlockSpec((CHUNK, nq), lambda b, c: (b * nc + c, 0)),
        out_shape=jax.ShapeDtypeStruct((batch * seq, nq), BF16),
        compiler_params=_cparams(("arbitrary", "arbitrary")),
        name="win_prompt",
    )(sinks, qkv, qkv, qkv, qkv, qkv, qkv, qkv)


def _win_sample_kernel(sinks_ref, q_ref, kn_ref, vn_ref, ck_ref, cv_ref, o_ref, *, n_kv):
    t = kn_ref.shape[0]
    pad = jnp.zeros((WIN_KEYS_PAD - WINDOW - t, kn_ref.shape[1]), F32)
    k = jnp.concatenate([ck_ref[0], kn_ref[...], pad], axis=0)
    v = jnp.concatenate([cv_ref[0], vn_ref[...], pad], axis=0)
    o_ref[...] = _sink_attention(q_ref[...], k, v, sinks_ref, 0, WINDOW + t, n_kv).astype(o_ref.dtype)


def _win_sample(qkv, cache_k, cache_v, sinks, row0, batch, t, n_kv):
    nq = n_kv * GROUP_A * HEAD_DIM_A
    nk = n_kv * HEAD_DIM_A
    kcol = nq // nk
    blk0 = row0 // t
    return pl.pallas_call(
        functools.partial(_win_sample_kernel, n_kv=n_kv),
        grid=(batch,),
        in_specs=[pl.BlockSpec(memory_space=pltpu.SMEM),
                  pl.BlockSpec((t, nq), lambda b: (blk0 + b, 0)),
                  pl.BlockSpec((t, nk), lambda b: (blk0 + b, kcol)),
                  pl.BlockSpec((t, nk), lambda b: (blk0 + b, kcol + 1)),
                  pl.BlockSpec((1, WINDOW, nk), lambda b: (b, 0, 0)),
                  pl.BlockSpec((1, WINDOW, nk), lambda b: (b, 0, 0))],
        out_specs=pl.BlockSpec((t, nq), lambda b: (b, 0)),
        out_shape=jax.ShapeDtypeStruct((batch * t, nq), BF16),
        compiler_params=_cparams(("arbitrary",)),
        name="win_sample",
    )(sinks, qkv, qkv, qkv, cache_k, cache_v)


SB_STOP_BELOW = -110.0
SB_BLOCK = 256


def _sb_weights(z, mask, carry, u):
    sp = jnp.maximum(z, 0.0) + jnp.log(1.0 + jnp.exp(-jnp.abs(z)))
    ls = -sp if mask is None else jnp.where(mask, -sp, 0.0)
    hi = ls.astype(BF16)
    lo = (ls - hi.astype(F32)).astype(BF16)
    after = (jnp.dot(hi, u, preferred_element_type=F32) + jnp.dot(lo, u, preferred_element_type=F32)) + carry
    a = jnp.exp(z - sp + after)
    if mask is not None:
        a = jnp.where(mask, a, 0.0)
    return a, carry + jnp.sum(ls, axis=-1, keepdims=True)


def _sb_prompt_kernel(q_ref, k_ref, v_ref, u_ref, o_ref, *, seq):
    tb = SB_BLOCK
    scale = HEAD_DIM_B ** -0.5
    u = u_ref[...]
    row = lax.broadcasted_iota(jnp.int32, (tb, tb), 0)
    col = lax.broadcasted_iota(jnp.int32, (tb, tb), 1)
    strictly_earlier = col < row
    no_carry = jnp.zeros((tb, 1), F32)

    def queries(start):
        return (q_ref[pl.ds(start, tb), :] * scale).astype(BF16)

    def scores(q, start, size):
        kb = k_ref[pl.ds(start, size), :].astype(BF16)
        return lax.dot_general(q, kb, NT_DIMS, preferred_element_type=F32)

    def weighted_values(a, start, size):
        vb = v_ref[pl.ds(start, size), :].astype(BF16)
        return jnp.dot(a.astype(BF16), vb, preferred_element_type=F32)

    a, _ = _sb_weights(scores(queries(0), 0, tb), strictly_earlier, no_carry, u)
    o_ref[pl.ds(0, tb), :] = weighted_values(a, 0, tb).astype(o_ref.dtype)

    def q_block(qi, _):
        q0 = pl.multiple_of(qi * tb, tb)
        p0 = pl.multiple_of(q0 - tb, tb)
        q = queries(q0)
        z = scores(q, p0, 2 * tb)
        a_own, carry = _sb_weights(z[:, tb:], strictly_earlier, no_carry, u)
        a_prev, carry = _sb_weights(z[:, :tb], None, carry, u)
        acc = weighted_values(jnp.concatenate([a_prev, a_own], axis=1), p0, 2 * tb)

        def more(state):
            kj, cmax, _, _ = state
            return jnp.logical_and(kj >= 0, cmax > SB_STOP_BELOW)

        def earlier_block(state):
            kj, _, carry, acc = state
            k0 = pl.multiple_of(kj * tb, tb)
            a, carry = _sb_weights(scores(q, k0, tb), None, carry, u)
            return kj - 1, jnp.max(carry), carry, acc + weighted_values(a, k0, tb)

        _, _, _, acc = lax.while_loop(more, earlier_block, (qi - 2, jnp.max(carry), carry, acc))
        o_ref[pl.ds(q0, tb), :] = acc.astype(o_ref.dtype)
        return 0

    lax.fori_loop(1, seq // tb, q_block, 0)


def _sb_prompt(qkv, u, batch, seq, n_heads):
    hd = HEAD_DIM_B
    assert seq % SB_BLOCK == 0
    return pl.pallas_call(
        functools.partial(_sb_prompt_kernel, seq=seq),
        grid=(batch, n_heads),
        in_specs=[pl.BlockSpec((seq, hd), lambda b, h: (b, h)),
                  pl.BlockSpec((seq, hd), lambda b, h: (b, n_heads + h)),
                  pl.BlockSpec((seq, hd), lambda b, h: (b, 2 * n_heads + h)),
                  pl.BlockSpec((SB_BLOCK, SB_BLOCK), lambda b, h: (0, 0))],
        out_specs=pl.BlockSpec((seq, hd), lambda b, h: (b, h)),
        out_shape=jax.ShapeDtypeStruct((batch * seq, n_heads * hd), BF16),
        compiler_params=_cparams(("arbitrary", "arbitrary")),
        name="sb_prompt",
    )(qkv, qkv, qkv, u)


SB_NEW_PAD = SB_BLOCK


def _sb_sample_kernel(q_ref, kn_ref, vn_ref, u_ref, ck_ref, cv_ref, o_ref, kbuf, vbuf, sem,
                      *, t, n_heads, n_blocks):
    b = pl.program_id(0)
    tb = SB_BLOCK
    hd = HEAD_DIM_B
    d = n_heads * hd
    rows = n_heads * t
    scale = hd ** -0.5
    u = u_ref[...]

    def block_copies(j, slot):
        start = pl.multiple_of((n_blocks - 1 - j) * tb, tb)
        out = []
        for h in range(n_heads):
            out.append(pltpu.make_async_copy(ck_ref.at[0, b, pl.ds(start, tb), h, :], kbuf.at[slot, h],
                                             sem.at[slot, 0]))
            out.append(pltpu.make_async_copy(cv_ref.at[0, b, pl.ds(start, tb), h, :], vbuf.at[slot, h],
                                             sem.at[slot, 1]))
        return out

    for n_c, c in enumerate(block_copies(0, 0)):
        c.start(priority=n_c % 2)

    q = q_ref[...] * scale
    qh = [q[:, h * hd:(h + 1) * hd].astype(BF16) for h in range(n_heads)]

    pad = jnp.zeros((SB_NEW_PAD - t, d), F32)
    kn = jnp.concatenate([kn_ref[...], pad], axis=0).astype(BF16)
    vn = jnp.concatenate([vn_ref[...], pad], axis=0).astype(BF16)
    z = jnp.concatenate(
        [lax.dot_general(qh[h], kn[:, h * hd:(h + 1) * hd], NT_DIMS, preferred_element_type=F32)
         for h in range(n_heads)], axis=0)
    rr = lax.broadcasted_iota(jnp.int32, (rows, SB_NEW_PAD), 0) % t
    cc = lax.broadcasted_iota(jnp.int32, (rows, SB_NEW_PAD), 1)
    a, carry = _sb_weights(z, cc < rr, jnp.zeros((rows, 1), F32), u[:SB_NEW_PAD, :SB_NEW_PAD])
    a = a.astype(BF16)
    acc = jnp.concatenate(
        [jnp.dot(a[h * t:(h + 1) * t], vn[:, h * hd:(h + 1) * hd], preferred_element_type=F32)
         for h in range(n_heads)], axis=1)

    def more(state):
        j, cmax, _, _ = state
        return jnp.logical_and(j < n_blocks, cmax > SB_STOP_BELOW)

    def cache_block(state):
        j, _, carry, acc = state
        slot = j % 2
        for c in block_copies(j, slot):
            c.wait()

        @pl.when(j + 1 < n_blocks)
        def _():
            for n_c, c in enumerate(block_copies(j + 1, 1 - slot)):
                c.start(priority=n_c % 2)

        z = jnp.concatenate(
            [lax.dot_general(qh[h], kbuf[slot, h].astype(BF16), NT_DIMS, preferred_element_type=F32)
             for h in range(n_heads)], axis=0)
        a, carry = _sb_weights(z, None, carry, u)
        a = a.astype(BF16)
        out = jnp.concatenate(
            [jnp.dot(a[h * t:(h + 1) * t], vbuf[slot, h].astype(BF16), preferred_element_type=F32)
             for h in range(n_heads)], axis=1)
        return j + 1, jnp.max(carry), carry, acc + out

    j, _, _, acc = lax.while_loop(more, cache_block, (jnp.int32(0), jnp.max(carry), carry, acc))
    o_ref[...] = acc.astype(o_ref.dtype)

    @pl.when(j < n_blocks)
    def _():
        for c in block_copies(j, j % 2):
            c.wait()


def _sb_sample(qkv, cache_k, cache_v, u, row0, batch, t, n_heads):
    hd = HEAD_DIM_B
    d = n_heads * hd
    past = cache_k.shape[2]
    assert past % SB_BLOCK == 0 and t <= SB_NEW_PAD
    blk0 = row0 // t
    return pl.pallas_call(
        functools.partial(_sb_sample_kernel, t=t, n_heads=n_heads, n_blocks=past // SB_BLOCK),
        grid=(batch,),
        in_specs=[pl.BlockSpec((t, d), lambda b: (blk0 + b, 0)),
                  pl.BlockSpec((t, d), lambda b: (blk0 + b, 1)),
                  pl.BlockSpec((t, d), lambda b: (blk0 + b, 2)),
                  pl.BlockSpec((SB_BLOCK, SB_BLOCK), lambda b: (0, 0)),
                  pl.BlockSpec(memory_space=pl.ANY),
                  pl.BlockSpec(memory_space=pl.ANY)],
        out_specs=pl.BlockSpec((t, d), lambda b: (b, 0)),
        out_shape=jax.ShapeDtypeStruct((batch * t, d), BF16),
        scratch_shapes=[pltpu.VMEM((2, n_heads, SB_BLOCK, hd), F32),
                        pltpu.VMEM((2, n_heads, SB_BLOCK, hd), F32),
                        pltpu.SemaphoreType.DMA((2, 2))],
        compiler_params=_cparams(("arbitrary",)),
        name="sb_sample",
    )(qkv, qkv, qkv, u, cache_k, cache_v)


def _pack_bf16_pair(a, b):
    ab = lax.bitcast_convert_type(a.astype(BF16).astype(F32), jnp.uint32)
    bb = lax.bitcast_convert_type(b.astype(BF16).astype(F32), jnp.uint32)
    return (ab >> 16) | (bb & jnp.uint32(0xFFFF0000))


def _unpack_bf16_pair(p):
    a = lax.bitcast_convert_type(p << 16, F32)
    b = lax.bitcast_convert_type(p & jnp.uint32(0xFFFF0000), F32)
    return a.astype(BF16), b.astype(BF16)


def _oproj_router_kernel(op_ref, os_ref, *refs, prompt_tiles, split_x):
    if split_x:
        x_ref, xs_ref, wo_ref, g_ref, wr_ref, br_ref, x1_ref, tp_ref, route_ref = refs
    else:
        x_ref, wo_ref, g_ref, wr_ref, br_ref, x1_ref, tp_ref, route_ref = refs
    in_prompt = pl.program_id(0) < prompt_tiles
    o = jnp.where(in_prompt, op_ref[...], os_ref[...])
    x = x_ref[...]
    if split_x:
        x = jnp.where(in_prompt, x, xs_ref[...])
    x1 = x + jnp.dot(o, wo_ref[...], preferred_element_type=F32)
    x1_ref[...] = x1
    ms = jnp.mean(x1 * x1, axis=-1, keepdims=True)
    t = x1 * lax.rsqrt(ms + RMS_EPS) * g_ref[...]
    half = t.shape[1] // 2
    tp_ref[...] = _pack_bf16_pair(t[:, :half], t[:, half:])
    hi = t.astype(BF16)
    lo = (t - hi.astype(F32)).astype(BF16)
    r_hi = jnp.dot(hi, wr_ref[...], preferred_element_type=F32)
    r_lo = jnp.dot(lo, wr_ref[...], preferred_element_type=F32)
    logits = r_hi + pltpu.roll(r_hi, LANES - ROUTER_LO_LANE, 1) + r_lo + br_ref[...]

    lane = lax.broadcasted_iota(jnp.int32, logits.shape, 1)
    lane_f = lane.astype(F32)
    big = float(LANES)
    is_group = lane < N_GROUPS
    gl = jnp.where(is_group, logits, -jnp.inf)
    gmax = jnp.max(gl, axis=-1, keepdims=True)
    gidx = jnp.min(jnp.where(gl == gmax, lane_f, big), axis=-1, keepdims=True)
    gsum = jnp.sum(jnp.where(is_group, jnp.exp(logits - gmax), 0.0), axis=-1, keepdims=True)
    g_w = 1.0 / gsum
    expert = lane - GATE_LANE0
    in_group = (expert >= 0) & (expert < N_EXPERTS) & ((expert // EXPERTS_PER_GROUP).astype(F32) == gidx)
    el = jnp.where(in_group, logits, -jnp.inf)
    v1 = jnp.max(el, axis=-1, keepdims=True)
    i1 = jnp.min(jnp.where(el == v1, lane_f, big), axis=-1, keepdims=True)
    el2 = jnp.where(lane_f == i1, -jnp.inf, el)
    v2 = jnp.max(el2, axis=-1, keepdims=True)
    i2 = jnp.min(jnp.where(el2 == v2, lane_f, big), axis=-1, keepdims=True)
    e21 = jnp.exp(v2 - v1)
    w1 = g_w / (1.0 + e21)
    w2 = g_w * e21 / (1.0 + e21)
    route_ref[...] = jnp.where(lane == 0, i1 - GATE_LANE0,
                               jnp.where(lane == 1, i2 - GATE_LANE0,
                                         jnp.where(lane == 2, w1, jnp.where(lane == 3, w2, 0.0))))


def _oproj_router(o_p, o_s, x, wo, g, wr, br):
    d = wo.shape[0]
    o_args, o_specs, tm, n, prompt_tiles = _row_specs((o_p, o_s), 256)
    if isinstance(x, tuple):
        x_args, x_specs = list(x), list(o_specs)
    else:
        x_args, x_specs = [x], [pl.BlockSpec((tm, d), lambda i: (i, 0))]
    row = lambda i: (i, 0)
    const = lambda i: (0, 0)
    return pl.pallas_call(
        functools.partial(_oproj_router_kernel, prompt_tiles=prompt_tiles, split_x=isinstance(x, tuple)),
        grid=(n // tm,),
        in_specs=o_specs + x_specs + [
                  pl.BlockSpec((d, d), const), pl.BlockSpec((1, d), const),
                  pl.BlockSpec((d, LANES), const), pl.BlockSpec((1, LANES), const)],
        out_specs=[pl.BlockSpec((tm, d), row), pl.BlockSpec((tm, d // 2), row), pl.BlockSpec((tm, LANES), row)],
        out_shape=[jax.ShapeDtypeStruct((n, d), F32), jax.ShapeDtypeStruct((n, d // 2), jnp.uint32),
                   jax.ShapeDtypeStruct((n, LANES), F32)],
        compiler_params=_cparams(("arbitrary",)),
        name="oproj_router",
    )(*o_args, *x_args, wo, g.reshape(1, d), wr, br)


MOE_TILE = 256
TOP_K = 2


def _moe_plan(route, n_tiles):
    n = route.shape[0]
    e_flat = route[:, :TOP_K].astype(jnp.int32).reshape(n * TOP_K)
    onehot = (e_flat[:, None] == jnp.arange(N_EXPERTS, dtype=jnp.int32)[None, :]).astype(jnp.int32)
    csum = jnp.cumsum(onehot, axis=0)
    rank = jnp.sum(csum * onehot, axis=1) - 1
    counts = csum[-1]
    tiles = (counts + MOE_TILE - 1) // MOE_TILE
    tile_end = jnp.cumsum(tiles)
    row_start = (tile_end - tiles) * MOE_TILE
    pos = jnp.sum(onehot * row_start[None, :], axis=1) + rank
    n_used = tile_end[-1]
    tile_id = jnp.minimum(jnp.arange(n_tiles, dtype=jnp.int32), n_used - 1)
    tile_expert = jnp.sum((tile_end[None, :] <= tile_id[:, None]).astype(jnp.int32), axis=1)
    return pos.astype(jnp.int32), tile_expert.astype(jnp.int32), n_used.reshape(1).astype(jnp.int32)


def _dispatch_kernel(pos_ref, tp_ref, xs_in_ref, xs_ref, sem, *, tc):
    del xs_in_ref
    base = pl.program_id(0) * tc

    def issue(j, carry):
        for s in range(TOP_K):
            p = pos_ref[TOP_K * (base + j) + s]
            pltpu.make_async_copy(tp_ref.at[pl.ds(j, 1)], xs_ref.at[pl.ds(p, 1)], sem).start(priority=s % 2)
        return carry

    lax.fori_loop(0, tc, issue, 0)
    for _ in range(TOP_K):
        pltpu.make_async_copy(tp_ref, xs_ref.at[pl.ds(0, tc)], sem).wait()


def _dispatch(pos, tp, n_rows_pad):
    n, dw = tp.shape
    tc = _pick(n, 1536)
    assert tc <= n_rows_pad
    grid_spec = pltpu.PrefetchScalarGridSpec(
        num_scalar_prefetch=1, grid=(n // tc,),
        in_specs=[pl.BlockSpec((tc, dw), lambda i, pos: (i, 0)), pl.BlockSpec(memory_space=pl.ANY)],
        out_specs=pl.BlockSpec(memory_space=pl.ANY),
        scratch_shapes=[pltpu.SemaphoreType.DMA(())])
    return pl.pallas_call(
        functools.partial(_dispatch_kernel, tc=tc),
        grid_spec=grid_spec,
        out_shape=jax.ShapeDtypeStruct((n_rows_pad, dw), jnp.uint32),
        input_output_aliases={2: 0},
        compiler_params=_cparams(("arbitrary",)),
        name="moe_dispatch",
    )(pos, tp, jnp.zeros((n_rows_pad, dw), jnp.uint32))


def _experts_kernel(te_ref, nu_ref, xs_ref, wg_ref, wu_ref, wd_ref, ys_ref, wg_s, wu_s, wd_s):
    i = pl.program_id(0)
    used = i < nu_ref[0]
    new_expert = jnp.logical_or(i == 0, te_ref[i] != te_ref[jnp.maximum(i - 1, 0)])

    @pl.when(jnp.logical_and(used, new_expert))
    def _():
        wg_s[...] = wg_ref[0, 0].astype(BF16)
        wu_s[...] = wu_ref[0, 0].astype(BF16)
        wd_s[...] = wd_ref[0, 0].astype(BF16)

    @pl.when(jnp.logical_not(used))
    def _():
        ys_ref[...] = jnp.zeros_like(ys_ref)

    @pl.when(used)
    def _():
        a, b = _unpack_bf16_pair(xs_ref[...])
        x = jnp.concatenate([a, b], axis=1)
        hg = jnp.dot(x, wg_s[...], preferred_element_type=F32)
        hu = jnp.dot(x, wu_s[...], preferred_element_type=F32)
        hid = (hg * jax.nn.sigmoid(hg) * hu).astype(BF16)
        ys_ref[...] = jnp.dot(hid, wd_s[...], preferred_element_type=F32)


def _experts(tile_expert, n_used, xs, wg, wu, wd, layer):
    rows, dw = xs.shape
    _, _, d, de = wg.shape
    grid_spec = pltpu.PrefetchScalarGridSpec(
        num_scalar_prefetch=2, grid=(rows // MOE_TILE,),
        in_specs=[pl.BlockSpec((MOE_TILE, dw), lambda i, te, nu: (i, 0)),
                  pl.BlockSpec((1, 1, d, de), lambda i, te, nu: (layer, te[i], 0, 0)),
                  pl.BlockSpec((1, 1, d, de), lambda i, te, nu: (layer, te[i], 0, 0)),
                  pl.BlockSpec((1, 1, de, d), lambda i, te, nu: (layer, te[i], 0, 0))],
        out_specs=pl.BlockSpec((MOE_TILE, d), lambda i, te, nu: (i, 0)),
        scratch_shapes=[pltpu.VMEM((d, de), BF16), pltpu.VMEM((d, de), BF16), pltpu.VMEM((de, d), BF16)])
    return pl.pallas_call(
        _experts_kernel,
        grid_spec=grid_spec,
        out_shape=jax.ShapeDtypeStruct((rows, d), F32),
        compiler_params=_cparams(("arbitrary",)),
        name="moe_experts",
    )(tile_expert, n_used, xs, wg, wu, wd)


COMBINE_ROWS = 32


def _combine_kernel(pos_ref, ys_ref, x1_ref, route_ref, *rest, tc, prompt_tiles):
    if prompt_tiles is None:
        o_ref, buf_ref, sem = rest
    else:
        g_ref, yp_ref, ysm_ref, buf_ref, sem = rest
    i = pl.program_id(0)
    n_steps = pl.num_programs(0)
    slot = i % 2

    def issue_rows(step, to_slot, row0, n_rows):
        for r in range(n_rows):
            for s in range(TOP_K):
                p = pos_ref[TOP_K * (step * tc + row0 + r) + s]
                pltpu.make_async_copy(ys_ref.at[pl.ds(p, 1)], buf_ref.at[to_slot, s, pl.ds(row0 + r, 1)],
                                      sem.at[to_slot]).start(priority=s % 2)

    @pl.when(i == 0)
    def _():
        lax.fori_loop(0, tc // COMBINE_ROWS,
                      lambda c, carry: (issue_rows(0, 0, c * COMBINE_ROWS, COMBINE_ROWS), carry)[1], 0)

    for s in range(TOP_K):
        pltpu.make_async_copy(ys_ref.at[pl.ds(0, tc)], buf_ref.at[slot, s], sem.at[slot]).wait()

    def run(prefetch, out_ref):
        def body(c, carry):
            r0 = pl.multiple_of(c * COMBINE_ROWS, COMBINE_ROWS)
            if prefetch:
                issue_rows(i + 1, 1 - slot, r0, COMBINE_ROWS)
            rows = pl.ds(r0, COMBINE_ROWS)
            route = route_ref[rows, :]
            x = (x1_ref[rows, :] + buf_ref[slot, 0, rows, :] * route[:, TOP_K:TOP_K + 1]
                 + buf_ref[slot, 1, rows, :] * route[:, TOP_K + 1:TOP_K + 2])
            if prompt_tiles is not None:
                ms = jnp.mean(x * x, axis=-1, keepdims=True)
                x = x * lax.rsqrt(ms + RMS_EPS) * g_ref[...]
            out_ref[rows, :] = x
            return carry
        lax.fori_loop(0, tc // COMBINE_ROWS, body, 0)

    more = i + 1 < n_steps
    for prefetch, cond in ((True, more), (False, jnp.logical_not(more))):
        if prompt_tiles is None:
            pl.when(cond)(functools.partial(run, prefetch, o_ref))
        else:
            pl.when(jnp.logical_and(cond, i < prompt_tiles))(functools.partial(run, prefetch, yp_ref))
            pl.when(jnp.logical_and(cond, i >= prompt_tiles))(functools.partial(run, prefetch, ysm_ref))


def _combine(pos, ys, x1, route, n_p, final_gain=None):
    n, d = x1.shape
    tc, prompt_tiles = _split_tiles(n_p, n - n_p, 256)
    assert tc % COMBINE_ROWS == 0
    row = lambda i, pos: (i, 0)
    in_specs = [pl.BlockSpec(memory_space=pl.ANY), pl.BlockSpec((tc, d), row), pl.BlockSpec((tc, LANES), row)]
    args = [pos, ys, x1, route]
    if final_gain is None:
        out_specs = pl.BlockSpec((tc, d), row)
        out_shape = jax.ShapeDtypeStruct((n, d), F32)
    else:
        in_specs.append(pl.BlockSpec((1, d), lambda i, pos: (0, 0)))
        args.append(final_gain.reshape(1, d))
        out_specs = [pl.BlockSpec((tc, d), lambda i, pos: (jnp.minimum(i, prompt_tiles - 1), 0)),
                     pl.BlockSpec((tc, d), lambda i, pos: (jnp.maximum(i - prompt_tiles, 0), 0))]
        out_shape = [jax.ShapeDtypeStruct((n_p, d), F32), jax.ShapeDtypeStruct((n - n_p, d), F32)]
    grid_spec = pltpu.PrefetchScalarGridSpec(
        num_scalar_prefetch=1, grid=(n // tc,), in_specs=in_specs, out_specs=out_specs,
        scratch_shapes=[pltpu.VMEM((2, TOP_K, tc, d), F32), pltpu.SemaphoreType.DMA((2,))])
    return pl.pallas_call(
        functools.partial(_combine_kernel, tc=tc, prompt_tiles=None if final_gain is None else prompt_tiles),
        grid_spec=grid_spec,
        out_shape=out_shape,
        compiler_params=_cparams(("arbitrary",)),
        name="moe_combine",
    )(*args)


def _moe(tp, route, x1, wg, wu, wd, layer, n_p, final_gain=None):
    n = x1.shape[0]
    n_tiles = -(-n * TOP_K // MOE_TILE) + N_EXPERTS
    pos, tile_expert, n_used = _moe_plan(route, n_tiles)
    xs = _dispatch(pos, tp, n_tiles * MOE_TILE)
    ys = _experts(tile_expert, n_used, xs, wg, wu, wd, layer)
    return _combine(pos, ys, x1, route, n_p, final_gain)


def _rope_tables(pos):
    half = HEAD_DIM_A // 2
    inv = ROPE_THETA ** (-jnp.arange(half, dtype=F32) / half)
    ang = pos.astype(F32)[:, None] * inv[None, :]
    cos = jnp.cos(ang)
    sin = jnp.sin(ang)
    reps = LANES // HEAD_DIM_A
    return jnp.tile(jnp.concatenate([cos, cos], axis=1), (1, reps)), \
        jnp.tile(jnp.concatenate([-sin, sin], axis=1), (1, reps))


def _router_weights(w_group, b_group, w_router, b_router):
    d = w_group.shape[0]
    w = jnp.concatenate([w_group, w_router], axis=1).astype(F32)
    n_log = w.shape[1]
    hi = w.astype(BF16)
    lo = (w - hi.astype(F32)).astype(BF16)
    wr = jnp.zeros((d, LANES), BF16)
    wr = wr.at[:, :n_log].set(hi).at[:, ROUTER_LO_LANE:ROUTER_LO_LANE + n_log].set(lo)
    br = jnp.zeros((1, LANES), F32).at[0, :n_log].set(jnp.concatenate([b_group, b_router]).astype(F32))
    return wr, br


def kernel(x_prompt, x_sample, cache_win_k, cache_win_v, cache_sb_k, cache_sb_v, norm_mix, norm_ffn, norm_final,
           a_w_qkv, a_b_qkv, a_sinks, a_w_o, b_w_qkv, b_w_o, moe_w_group, moe_b_group, moe_w_router,
           moe_b_router, moe_w_gate, moe_w_up, moe_w_down):
    bp, sp, d = x_prompt.shape
    bs, ts, _ = x_sample.shape
    n_p = bp * sp
    n_s = bs * ts
    past = cache_sb_k.shape[2]
    n_kv = cache_win_k.shape[3]
    n_heads_b = cache_sb_k.shape[3]
    nq_a = n_kv * GROUP_A * HEAD_DIM_A
    nk_a = n_kv * HEAD_DIM_A
    assert sp % CHUNK == 0 and past % CHUNK == 0 and ts <= CHUNK and n_p % ts == 0
    assert cache_win_k.shape[2] == WINDOW and d == n_heads_b * HEAD_DIM_B == nq_a

    x = (x_prompt.reshape(n_p, d), x_sample.reshape(n_s, d))
    pos = jnp.concatenate([jnp.tile(jnp.arange(sp, dtype=jnp.int32), bp),
                           jnp.tile(past + jnp.arange(ts, dtype=jnp.int32), bs)])
    cos, sin = _rope_tables(pos)
    u = (lax.broadcasted_iota(jnp.int32, (SB_BLOCK, SB_BLOCK), 0)
         > lax.broadcasted_iota(jnp.int32, (SB_BLOCK, SB_BLOCK), 1)).astype(BF16)

    outs = {}
    for i in range(2):
        if i == 0:
            tn = _pick(a_w_qkv.shape[2], 1280)
            qkv = _norm_proj(x, norm_mix[i], a_w_qkv[0].astype(BF16), a_b_qkv[0], cos, sin, nq_a + nk_a, tn)
            sinks = a_sinks[0].astype(F32)
            o_p = _win_prompt(qkv, sinks, bp, sp, n_kv)
            o_s = _win_sample(qkv, cache_win_k[0].reshape(bs, WINDOW, nk_a), cache_win_v[0].reshape(bs, WINDOW, nk_a),
                              sinks, n_p, bs, ts, n_kv)
            k_all = qkv[:, nq_a:nq_a + nk_a]
            v_all = qkv[:, nq_a + nk_a:]
            outs["wkp"] = k_all[:n_p].reshape(bp, sp, n_kv, HEAD_DIM_A)[:, sp - WINDOW:][None]
            outs["wvp"] = v_all[:n_p].reshape(bp, sp, n_kv, HEAD_DIM_A)[:, sp - WINDOW:][None]
            outs["wks"] = jnp.concatenate(
                [cache_win_k[0], k_all[n_p:].reshape(bs, ts, n_kv, HEAD_DIM_A)], axis=1)[:, -WINDOW:][None]
            outs["wvs"] = jnp.concatenate(
                [cache_win_v[0], v_all[n_p:].reshape(bs, ts, n_kv, HEAD_DIM_A)], axis=1)[:, -WINDOW:][None]
            w_o = a_w_o[0]
        else:
            zeros_b = jnp.zeros((b_w_qkv.shape[2],), F32)
            qkv = _norm_proj(x, norm_mix[i], b_w_qkv[0].astype(BF16), zeros_b, cos, sin, 0, d)
            o_p = _sb_prompt(qkv, u, bp, sp, n_heads_b)
            o_s = _sb_sample(qkv, cache_sb_k, cache_sb_v, u, n_p, bs, ts, n_heads_b)
            k_all = qkv[:, d:2 * d]
            v_all = qkv[:, 2 * d:]
            outs["skp"] = k_all[:n_p].reshape(1, bp, sp, n_heads_b, HEAD_DIM_B)
            outs["svp"] = v_all[:n_p].reshape(1, bp, sp, n_heads_b, HEAD_DIM_B)
            outs["sks"] = k_all[n_p:].reshape(1, bs, ts, n_heads_b, HEAD_DIM_B)
            outs["svs"] = v_all[n_p:].reshape(1, bs, ts, n_heads_b, HEAD_DIM_B)
            w_o = b_w_o[0]
        wr, br = _router_weights(moe_w_group[i], moe_b_group[i], moe_w_router[i], moe_b_router[i])
        x1, tp, route = _oproj_router(o_p, o_s, x, w_o.astype(BF16), norm_ffn[i], wr, br)
        x = _moe(tp, route, x1, moe_w_gate, moe_w_up, moe_w_down, i, n_p,
                 final_gain=norm_final if i == 1 else None)
    y_p, y_s = x
    return (y_p.reshape(bp, sp, d), y_s.reshape(bs, ts, d),
            outs["wkp"], outs["wvp"], outs["wks"], outs["wvs"],
            outs["skp"], outs["svp"], outs["sks"], outs["svs"])
```

```python
import functools
import math

import jax
import jax.numpy as jnp
from jax import lax
from jax.experimental import pallas as pl
from jax.experimental.pallas import tpu as pltpu

F32 = jnp.float32
BF16 = jnp.bfloat16

CHUNK = 64
WINDOW = 128
HEAD_DIM_A = 64
GROUP_A = 8
HEAD_DIM_B = 128
N_GROUPS = 4
EXPERTS_PER_GROUP = 8
N_EXPERTS = N_GROUPS * EXPERTS_PER_GROUP
ROPE_THETA = 10000.0
RMS_EPS = 1e-6
NEG_INF = -1e30

LANES = 128
ROUTER_LO_LANE = 64
GATE_LANE0 = N_GROUPS
VMEM_LIMIT = 56 * 1024 * 1024

NT_DIMS = (((1,), (1,)), ((), ()))


def _cparams(sem):
    return pltpu.CompilerParams(dimension_semantics=sem, vmem_limit_bytes=VMEM_LIMIT)


def _pick(n, pref, align=8):
    for t in range(min(n, pref), 0, -1):
        if n % t == 0 and t % align == 0:
            return t
    raise ValueError((n, pref, align))


def _norm_proj_kernel(*refs, rope_cols, tn, prompt_tiles):
    if prompt_tiles is None:
        x_ref, g_ref, w_ref, b_ref, cos_ref, sin_ref, o_ref, h_ref = refs
    else:
        x_ref, xs_ref, g_ref, w_ref, b_ref, cos_ref, sin_ref, o_ref, h_ref = refs
    j = pl.program_id(1)

    @pl.when(j == 0)
    def _():
        x = x_ref[...]
        if prompt_tiles is not None:
            x = jnp.where(pl.program_id(0) < prompt_tiles, x, xs_ref[...])
        ms = jnp.mean(x * x, axis=-1, keepdims=True)
        h_ref[...] = (x * lax.rsqrt(ms + RMS_EPS) * g_ref[...]).astype(BF16)

    y = jnp.dot(h_ref[...], w_ref[...], preferred_element_type=F32) + b_ref[...]
    if rope_cols:
        reps = tn // LANES
        cos = jnp.concatenate([cos_ref[...]] * reps, axis=1)
        sin = jnp.concatenate([sin_ref[...]] * reps, axis=1)
        lane = lax.broadcasted_iota(jnp.int32, y.shape, 1)
        half = HEAD_DIM_A // 2
        first = (lane % HEAD_DIM_A) < half
        swapped = jnp.where(first, pltpu.roll(y, tn - half, 1), pltpu.roll(y, half, 1))
        roped = y * cos + swapped * sin
        y = jnp.where(lane + j * tn < rope_cols, roped, y)
    o_ref[...] = y


def _split_tiles(n_p, n_s, pref):
    tm = _pick(math.gcd(n_p, n_s), pref, 16)
    return tm, n_p // tm


def _row_specs(x, pref):
    if not isinstance(x, tuple):
        n, d = x.shape
        tm = _pick(n, pref)
        return [x], [pl.BlockSpec((tm, d), lambda i, *_: (i, 0))], tm, n, None
    x_p, x_s = x
    d = x_p.shape[1]
    tm, prompt_tiles = _split_tiles(x_p.shape[0], x_s.shape[0], pref)
    specs = [pl.BlockSpec((tm, d), lambda i, *_: (jnp.minimum(i, prompt_tiles - 1), 0)),
             pl.BlockSpec((tm, d), lambda i, *_: (jnp.maximum(i - prompt_tiles, 0), 0))]
    return [x_p, x_s], specs, tm, x_p.shape[0] + x_s.shape[0], prompt_tiles


def _norm_proj(x, g, w, b, cos, sin, rope_cols, tn):
    x_args, x_specs, tm, n, prompt_tiles = _row_specs(x, 512)
    d, nout = w.shape
    kern = functools.partial(_norm_proj_kernel, rope_cols=rope_cols, tn=tn, prompt_tiles=prompt_tiles)
    return pl.pallas_call(
        kern,
        grid=(n // tm, nout // tn),
        in_specs=x_specs + [
            pl.BlockSpec((1, d), lambda i, j: (0, 0)),
            pl.BlockSpec((d, tn), lambda i, j: (0, j)),
            pl.BlockSpec((1, tn), lambda i, j: (0, j)),
            pl.BlockSpec((tm, LANES), lambda i, j: (i, 0)),
            pl.BlockSpec((tm, LANES), lambda i, j: (i, 0)),
        ],
        out_specs=pl.BlockSpec((tm, tn), lambda i, j: (i, j)),
        out_shape=jax.ShapeDtypeStruct((n, nout), F32),
        scratch_shapes=[pltpu.VMEM((tm, d), BF16)],
        compiler_params=_cparams(("arbitrary", "arbitrary")),
        name="norm_proj",
    )(*x_args, g.reshape(1, d), w, b.reshape(1, nout), cos, sin)


WIN_KEYS_PAD = 256
HALF = LANES // 2


def _sink_attention(q, k, v, sinks_ref, lo, n_keys, n_kv):
    t = q.shape[0]
    sp = WIN_KEYS_PAD
    assert HEAD_DIM_A == HALF and n_kv % 2 == 0 and n_keys < sp
    qb = (q * (HEAD_DIM_A ** -0.5)).astype(BF16)
    low_half = lax.broadcasted_iota(jnp.int32, (sp, LANES), 1) < HALF
    halves = (low_half, jnp.logical_not(low_half))
    ones = tuple(jnp.where(h, 1.0, 0.0).astype(BF16) for h in halves)
    col = lax.broadcasted_iota(jnp.int32, (1, sp), 1)
    visible = jnp.logical_and(col >= lo, col < n_keys)
    is_sink = col == n_keys

    kmat, vmat = {}, {}
    for slab in range(n_kv // 2):
        ks = k[:, slab * LANES:(slab + 1) * LANES]
        vs = v[:, slab * LANES:(slab + 1) * LANES]
        moved = (pltpu.roll(ks, HALF, 1), pltpu.roll(vs, HALF, 1))
        for gp in range(2):
            for par in range(2):
                src_k, src_v = (ks, vs) if par == gp else moved
                kmat[2 * slab + gp, par] = jnp.where(halves[par], src_k, 0.0).astype(BF16)
                vmat[2 * slab + gp, par] = jnp.concatenate(
                    [jnp.where(halves[par], src_v, 0.0).astype(BF16), ones[par]], axis=1)

    ppg = GROUP_A // 2
    outs = []
    for g in range(n_kv):
        qg = jnp.concatenate([qb[:, (g * ppg + i) * LANES:(g * ppg + i + 1) * LANES] for i in range(ppg)], axis=0)
        acc = None
        for par in range(2):
            s = lax.dot_general(qg, kmat[g, par], NT_DIMS, preferred_element_type=F32)
            s = jnp.where(visible, s, NEG_INF)
            s = jnp.concatenate(
                [jnp.where(is_sink, sinks_ref[2 * (g * ppg + i) + par], s[i * t:(i + 1) * t]) for i in range(ppg)],
                axis=0)
            p = jnp.exp(s - jnp.max(s, axis=-1, keepdims=True)).astype(BF16)
            o = jnp.dot(p, vmat[g, par], preferred_element_type=F32)
            acc = o if acc is None else acc + o
        o_norm = acc[:, :LANES] / acc[:, LANES:]
        outs.extend(o_norm[i * t:(i + 1) * t] for i in range(ppg))
    return jnp.concatenate(outs, axis=1)


def _win_prompt_kernel(sinks_ref, q_ref, k0_ref, k1_ref, k2_ref, v0_ref, v1_ref, v2_ref, o_ref, *, n_kv):
    c = pl.program_id(1)
    pad = jnp.zeros((WIN_KEYS_PAD - 3 * CHUNK, k0_ref.shape[1]), F32)
    k = jnp.concatenate([k0_ref[...], k1_ref[...], k2_ref[...], pad], axis=0)
    v = jnp.concatenate([v0_ref[...], v1_ref[...], v2_ref[...], pad], axis=0)
    lo = jnp.maximum(2 - c, 0) * CHUNK
    o_ref[...] = _sink_attention(q_ref[...], k, v, sinks_ref, lo, 3 * CHUNK, n_kv).astype(o_ref.dtype)


def _win_prompt(qkv, sinks, batch, seq, n_kv):
    nq = n_kv * GROUP_A * HEAD_DIM_A
    nk = n_kv * HEAD_DIM_A
    nc = seq // CHUNK
    kcol = nq // nk
    q_spec = pl.BlockSpec((CHUNK, nq), lambda b, c: (b * nc + c, 0))

    def kv_spec(back, col):
        return pl.BlockSpec((CHUNK, nk), lambda b, c: (b * nc + jnp.maximum(c - back, 0), col))

    return pl.pallas_call(
        functools.partial(_win_prompt_kernel, n_kv=n_kv),
        grid=(batch, nc),
        in_specs=[pl.BlockSpec(memory_space=pltpu.SMEM),
                  q_spec, kv_spec(2, kcol), kv_spec(1, kcol), kv_spec(0, kcol),
                  kv_spec(2, kcol + 1), kv_spec(1, kcol + 1), kv_spec(0, kcol + 1)],
        out_specs=pl.BlockSpec((CHUNK, nq), lambda b, c: (b * nc + c, 0)),
        out_shape=jax.ShapeDtypeStruct((batch * seq, nq), BF16),
        compiler_params=_cparams(("arbitrary", "arbitrary")),
        name="win_prompt",
    )(sinks, qkv, qkv, qkv, qkv, qkv, qkv, qkv)


def _win_sample_kernel(sinks_ref, q_ref, kn_ref, vn_ref, ck_ref, cv_ref, o_ref, *, n_kv):
    t = kn_ref.shape[0]
    pad = jnp.zeros((WIN_KEYS_PAD - WINDOW - t, kn_ref.shape[1]), F32)
    k = jnp.concatenate([ck_ref[0], kn_ref[...], pad], axis=0)
    v = jnp.concatenate([cv_ref[0], vn_ref[...], pad], axis=0)
    o_ref[...] = _sink_attention(q_ref[...], k, v, sinks_ref, 0, WINDOW + t, n_kv).astype(o_ref.dtype)


def _win_sample(qkv, cache_k, cache_v, sinks, row0, batch, t, n_kv):
    nq = n_kv * GROUP_A * HEAD_DIM_A
    nk = n_kv * HEAD_DIM_A
    kcol = nq // nk
    blk0 = row0 // t
    return pl.pallas_call(
        functools.partial(_win_sample_kernel, n_kv=n_kv),
        grid=(batch,),
        in_specs=[pl.BlockSpec(memory_space=pltpu.SMEM),
                  pl.BlockSpec((t, nq), lambda b: (blk0 + b, 0)),
                  pl.BlockSpec((t, nk), lambda b: (blk0 + b, kcol)),
                  pl.BlockSpec((t, nk), lambda b: (blk0 + b, kcol + 1)),
                  pl.BlockSpec((1, WINDOW, nk), lambda b: (b, 0, 0)),
                  pl.BlockSpec((1, WINDOW, nk), lambda b: (b, 0, 0))],
        out_specs=pl.BlockSpec((t, nq), lambda b: (b, 0)),
        out_shape=jax.ShapeDtypeStruct((batch * t, nq), BF16),
        compiler_params=_cparams(("arbitrary",)),
        name="win_sample",
    )(sinks, qkv, qkv, qkv, cache_k, cache_v)


SB_STOP_BELOW = -110.0
SB_BLOCK = 256


def _sb_weights(z, mask, carry, u):
    sp = jnp.maximum(z, 0.0) + jnp.log(1.0 + jnp.exp(-jnp.abs(z)))
    ls = -sp if mask is None else jnp.where(mask, -sp, 0.0)
    hi = ls.astype(BF16)
    lo = (ls - hi.astype(F32)).astype(BF16)
    after = (jnp.dot(hi, u, preferred_element_type=F32) + jnp.dot(lo, u, preferred_element_type=F32)) + carry
    a = jnp.exp(z - sp + after)
    if mask is not None:
        a = jnp.where(mask, a, 0.0)
    return a, carry + jnp.sum(ls, axis=-1, keepdims=True)


def _sb_prompt_kernel(q_ref, k_ref, v_ref, u_ref, o_ref, *, seq):
    tb = SB_BLOCK
    scale = HEAD_DIM_B ** -0.5
    u = u_ref[...]
    row = lax.broadcasted_iota(jnp.int32, (tb, tb), 0)
    col = lax.broadcasted_iota(jnp.int32, (tb, tb), 1)
    strictly_earlier = col < row
    no_carry = jnp.zeros((tb, 1), F32)

    def queries(start):
        return (q_ref[pl.ds(start, tb), :] * scale).astype(BF16)

    def scores(q, start, size):
        kb = k_ref[pl.ds(start, size), :].astype(BF16)
        return lax.dot_general(q, kb, NT_DIMS, preferred_element_type=F32)

    def weighted_values(a, start, size):
        vb = v_ref[pl.ds(start, size), :].astype(BF16)
        return jnp.dot(a.astype(BF16), vb, preferred_element_type=F32)

    a, _ = _sb_weights(scores(queries(0), 0, tb), strictly_earlier, no_carry, u)
    o_ref[pl.ds(0, tb), :] = weighted_values(a, 0, tb).astype(o_ref.dtype)

    def q_block(qi, _):
        q0 = pl.multiple_of(qi * tb, tb)
        p0 = pl.multiple_of(q0 - tb, tb)
        q = queries(q0)
        z = scores(q, p0, 2 * tb)
        a_own, carry = _sb_weights(z[:, tb:], strictly_earlier, no_carry, u)
        a_prev, carry = _sb_weights(z[:, :tb], None, carry, u)
        acc = weighted_values(jnp.concatenate([a_prev, a_own], axis=1), p0, 2 * tb)

        def more(state):
            kj, cmax, _, _ = state
            return jnp.logical_and(kj >= 0, cmax > SB_STOP_BELOW)

        def earlier_block(state):
            kj, _, carry, acc = state
            k0 = pl.multiple_of(kj * tb, tb)
            a, carry = _sb_weights(scores(q, k0, tb), None, carry, u)
            return kj - 1, jnp.max(carry), carry, acc + weighted_values(a, k0, tb)

        _, _, _, acc = lax.while_loop(more, earlier_block, (qi - 2, jnp.max(carry), carry, acc))
        o_ref[pl.ds(q0, tb), :] = acc.astype(o_ref.dtype)
        return 0

    lax.fori_loop(1, seq // tb, q_block, 0)


def _sb_prompt(qkv, u, batch, seq, n_heads):
    hd = HEAD_DIM_B
    assert seq % SB_BLOCK == 0
    return pl.pallas_call(
        functools.partial(_sb_prompt_kernel, seq=seq),
        grid=(batch, n_heads),
        in_specs=[pl.BlockSpec((seq, hd), lambda b, h: (b, h)),
                  pl.BlockSpec((seq, hd), lambda b, h: (b, n_heads + h)),
                  pl.BlockSpec((seq, hd), lambda b, h: (b, 2 * n_heads + h)),
                  pl.BlockSpec((SB_BLOCK, SB_BLOCK), lambda b, h: (0, 0))],
        out_specs=pl.BlockSpec((seq, hd), lambda b, h: (b, h)),
        out_shape=jax.ShapeDtypeStruct((batch * seq, n_heads * hd), BF16),
        compiler_params=_cparams(("arbitrary", "arbitrary")),
        name="sb_prompt",
    )(qkv, qkv, qkv, u)


SB_NEW_PAD = SB_BLOCK


def _sb_sample_kernel(q_ref, kn_ref, vn_ref, u_ref, ck_ref, cv_ref, o_ref, kbuf, vbuf, sem,
                      *, t, n_heads, n_blocks):
    b = pl.program_id(0)
    tb = SB_BLOCK
    hd = HEAD_DIM_B
    d = n_heads * hd
    rows = n_heads * t
    scale = hd ** -0.5
    u = u_ref[...]

    def block_copies(j, slot, batch=b):
        start = pl.multiple_of((n_blocks - 1 - j) * tb, tb)
        out = []
        for h in range(n_heads):
            out.append(pltpu.make_async_copy(ck_ref.at[0, batch, pl.ds(start, tb), h, :], kbuf.at[slot, h],
                                             sem.at[slot, 0]))
            out.append(pltpu.make_async_copy(cv_ref.at[0, batch, pl.ds(start, tb), h, :], vbuf.at[slot, h],
                                             sem.at[slot, 1]))
        return out

    @pl.when(b == 0)
    def _():
        for c in block_copies(0, 0):
            c.start()

    q = q_ref[...] * scale
    qh = [q[:, h * hd:(h + 1) * hd].astype(BF16) for h in range(n_heads)]

    pad = jnp.zeros((SB_NEW_PAD - t, d), F32)
    kn = jnp.concatenate([kn_ref[...], pad], axis=0).astype(BF16)
    vn = jnp.concatenate([vn_ref[...], pad], axis=0).astype(BF16)
    z = jnp.concatenate(
        [lax.dot_general(qh[h], kn[:, h * hd:(h + 1) * hd], NT_DIMS, preferred_element_type=F32)
         for h in range(n_heads)], axis=0)
    rr = lax.broadcasted_iota(jnp.int32, (rows, SB_NEW_PAD), 0) % t
    cc = lax.broadcasted_iota(jnp.int32, (rows, SB_NEW_PAD), 1)
    a, carry = _sb_weights(z, cc < rr, jnp.zeros((rows, 1), F32), u[:SB_NEW_PAD, :SB_NEW_PAD])
    a = a.astype(BF16)
    acc = jnp.concatenate(
        [jnp.dot(a[h * t:(h + 1) * t], vn[:, h * hd:(h + 1) * hd], preferred_element_type=F32)
         for h in range(n_heads)], axis=1)

    def more(state):
        j, cmax, _, _ = state
        return jnp.logical_and(j < n_blocks, cmax > SB_STOP_BELOW)

    def cache_block(state):
        j, _, carry, acc = state
        slot = j % 2
        for c in block_copies(j, slot):
            c.wait()

        @pl.when(j + 1 < n_blocks)
        def _():
            for c in block_copies(j + 1, 1 - slot):
                c.start()

        z = jnp.concatenate(
            [lax.dot_general(qh[h], kbuf[slot, h].astype(BF16), NT_DIMS, preferred_element_type=F32)
             for h in range(n_heads)], axis=0)
        a, carry = _sb_weights(z, None, carry, u)
        a = a.astype(BF16)
        out = jnp.concatenate(
            [jnp.dot(a[h * t:(h + 1) * t], vbuf[slot, h].astype(BF16), preferred_element_type=F32)
             for h in range(n_heads)], axis=1)
        return j + 1, jnp.max(carry), carry, acc + out

    j, _, _, acc = lax.while_loop(more, cache_block, (jnp.int32(0), jnp.max(carry), carry, acc))
    o_ref[...] = acc.astype(o_ref.dtype)

    @pl.when(j < n_blocks)
    def _():
        for c in block_copies(j, j % 2):
            c.wait()

    @pl.when(b + 1 < pl.num_programs(0))
    def _():
        for c in block_copies(0, 0, b + 1):
            c.start()


def _sb_sample(qkv, cache_k, cache_v, u, row0, batch, t, n_heads):
    hd = HEAD_DIM_B
    d = n_heads * hd
    past = cache_k.shape[2]
    assert past % SB_BLOCK == 0 and t <= SB_NEW_PAD
    blk0 = row0 // t
    return pl.pallas_call(
        functools.partial(_sb_sample_kernel, t=t, n_heads=n_heads, n_blocks=past // SB_BLOCK),
        grid=(batch,),
        in_specs=[pl.BlockSpec((t, d), lambda b: (blk0 + b, 0)),
                  pl.BlockSpec((t, d), lambda b: (blk0 + b, 1)),
                  pl.BlockSpec((t, d), lambda b: (blk0 + b, 2)),
                  pl.BlockSpec((SB_BLOCK, SB_BLOCK), lambda b: (0, 0)),
                  pl.BlockSpec(memory_space=pl.ANY),
                  pl.BlockSpec(memory_space=pl.ANY)],
        out_specs=pl.BlockSpec((t, d), lambda b: (b, 0)),
        out_shape=jax.ShapeDtypeStruct((batch * t, d), BF16),
        scratch_shapes=[pltpu.VMEM((2, n_heads, SB_BLOCK, hd), F32),
                        pltpu.VMEM((2, n_heads, SB_BLOCK, hd), F32),
                        pltpu.SemaphoreType.DMA((2, 2))],
        compiler_params=_cparams(("arbitrary",)),
        name="sb_sample",
    )(qkv, qkv, qkv, u, cache_k, cache_v)


def _pack_bf16_pair(a, b):
    ab = lax.bitcast_convert_type(a.astype(BF16).astype(F32), jnp.uint32)
    bb = lax.bitcast_convert_type(b.astype(BF16).astype(F32), jnp.uint32)
    return (ab >> 16) | (bb & jnp.uint32(0xFFFF0000))


def _unpack_bf16_pair(p):
    a = lax.bitcast_convert_type(p << 16, F32)
    b = lax.bitcast_convert_type(p & jnp.uint32(0xFFFF0000), F32)
    return a.astype(BF16), b.astype(BF16)


def _oproj_router_kernel(op_ref, os_ref, *refs, prompt_tiles, split_x):
    if split_x:
        x_ref, xs_ref, wo_ref, g_ref, wr_ref, br_ref, x1_ref, tp_ref, route_ref = refs
    else:
        x_ref, wo_ref, g_ref, wr_ref, br_ref, x1_ref, tp_ref, route_ref = refs
    in_prompt = pl.program_id(0) < prompt_tiles
    o = jnp.where(in_prompt, op_ref[...], os_ref[...])
    x = x_ref[...]
    if split_x:
        x = jnp.where(in_prompt, x, xs_ref[...])
    x1 = x + jnp.dot(o, wo_ref[...], preferred_element_type=F32)
    x1_ref[...] = x1
    ms = jnp.mean(x1 * x1, axis=-1, keepdims=True)
    t = x1 * lax.rsqrt(ms + RMS_EPS) * g_ref[...]
    half = t.shape[1] // 2
    tp_ref[...] = _pack_bf16_pair(t[:, :half], t[:, half:])
    hi = t.astype(BF16)
    lo = (t - hi.astype(F32)).astype(BF16)
    r_hi = jnp.dot(hi, wr_ref[...], preferred_element_type=F32)
    r_lo = jnp.dot(lo, wr_ref[...], preferred_element_type=F32)
    logits = r_hi + pltpu.roll(r_hi, LANES - ROUTER_LO_LANE, 1) + r_lo + br_ref[...]

    lane = lax.broadcasted_iota(jnp.int32, logits.shape, 1)
    lane_f = lane.astype(F32)
    big = float(LANES)
    is_group = lane < N_GROUPS
    gl = jnp.where(is_group, logits, -jnp.inf)
    gmax = jnp.max(gl, axis=-1, keepdims=True)
    gidx = jnp.min(jnp.where(gl == gmax, lane_f, big), axis=-1, keepdims=True)
    gsum = jnp.sum(jnp.where(is_group, jnp.exp(logits - gmax), 0.0), axis=-1, keepdims=True)
    g_w = 1.0 / gsum
    expert = lane - GATE_LANE0
    in_group = (expert >= 0) & (expert < N_EXPERTS) & ((expert // EXPERTS_PER_GROUP).astype(F32) == gidx)
    el = jnp.where(in_group, logits, -jnp.inf)
    v1 = jnp.max(el, axis=-1, keepdims=True)
    i1 = jnp.min(jnp.where(el == v1, lane_f, big), axis=-1, keepdims=True)
    el2 = jnp.where(lane_f == i1, -jnp.inf, el)
    v2 = jnp.max(el2, axis=-1, keepdims=True)
    i2 = jnp.min(jnp.where(el2 == v2, lane_f, big), axis=-1, keepdims=True)
    e21 = jnp.exp(v2 - v1)
    w1 = g_w / (1.0 + e21)
    w2 = g_w * e21 / (1.0 + e21)
    route_ref[...] = jnp.where(lane == 0, i1 - GATE_LANE0,
                               jnp.where(lane == 1, i2 - GATE_LANE0,
                                         jnp.where(lane == 2, w1, jnp.where(lane == 3, w2, 0.0))))


def _oproj_router(o_p, o_s, x, wo, g, wr, br):
    d = wo.shape[0]
    o_args, o_specs, tm, n, prompt_tiles = _row_specs((o_p, o_s), 256)
    if isinstance(x, tuple):
        x_args, x_specs = list(x), list(o_specs)
    else:
        x_args, x_specs = [x], [pl.BlockSpec((tm, d), lambda i: (i, 0))]
    row = lambda i: (i, 0)
    const = lambda i: (0, 0)
    return pl.pallas_call(
        functools.partial(_oproj_router_kernel, prompt_tiles=prompt_tiles, split_x=isinstance(x, tuple)),
        grid=(n // tm,),
        in_specs=o_specs + x_specs + [
                  pl.BlockSpec((d, d), const), pl.BlockSpec((1, d), const),
                  pl.BlockSpec((d, LANES), const), pl.BlockSpec((1, LANES), const)],
        out_specs=[pl.BlockSpec((tm, d), row), pl.BlockSpec((tm, d // 2), row), pl.BlockSpec((tm, LANES), row)],
        out_shape=[jax.ShapeDtypeStruct((n, d), F32), jax.ShapeDtypeStruct((n, d // 2), jnp.uint32),
                   jax.ShapeDtypeStruct((n, LANES), F32)],
        compiler_params=_cparams(("arbitrary",)),
        name="oproj_router",
    )(*o_args, *x_args, wo, g.reshape(1, d), wr, br)


MOE_TILE = 256
TOP_K = 2


def _moe_plan(route, n_tiles):
    n = route.shape[0]
    e_flat = route[:, :TOP_K].astype(jnp.int32).reshape(n * TOP_K)
    onehot = (e_flat[:, None] == jnp.arange(N_EXPERTS, dtype=jnp.int32)[None, :]).astype(jnp.int32)
    csum = jnp.cumsum(onehot, axis=0)
    rank = jnp.sum(csum * onehot, axis=1) - 1
    counts = csum[-1]
    tiles = (counts + MOE_TILE - 1) // MOE_TILE
    tile_end = jnp.cumsum(tiles)
    row_start = (tile_end - tiles) * MOE_TILE
    pos = jnp.sum(onehot * row_start[None, :], axis=1) + rank
    n_used = tile_end[-1]
    tile_id = jnp.minimum(jnp.arange(n_tiles, dtype=jnp.int32), n_used - 1)
    tile_expert = jnp.sum((tile_end[None, :] <= tile_id[:, None]).astype(jnp.int32), axis=1)
    return pos.astype(jnp.int32), tile_expert.astype(jnp.int32), n_used.reshape(1).astype(jnp.int32)


def _dispatch_kernel(pos_ref, tp_ref, xs_in_ref, xs_ref, sem, *, tc):
    del xs_in_ref
    base = pl.program_id(0) * tc

    def issue(j, carry):
        for s in range(TOP_K):
            p = pos_ref[TOP_K * (base + j) + s]
            pltpu.make_async_copy(tp_ref.at[pl.ds(j, 1)], xs_ref.at[pl.ds(p, 1)], sem).start()
        return carry

    lax.fori_loop(0, tc, issue, 0)
    for _ in range(TOP_K):
        pltpu.make_async_copy(tp_ref, xs_ref.at[pl.ds(0, tc)], sem).wait()


def _dispatch(pos, tp, n_rows_pad):
    n, dw = tp.shape
    tc = _pick(n, 1536)
    assert tc <= n_rows_pad
    grid_spec = pltpu.PrefetchScalarGridSpec(
        num_scalar_prefetch=1, grid=(n // tc,),
        in_specs=[pl.BlockSpec((tc, dw), lambda i, pos: (i, 0)), pl.BlockSpec(memory_space=pl.ANY)],
        out_specs=pl.BlockSpec(memory_space=pl.ANY),
        scratch_shapes=[pltpu.SemaphoreType.DMA(())])
    return pl.pallas_call(
        functools.partial(_dispatch_kernel, tc=tc),
        grid_spec=grid_spec,
        out_shape=jax.ShapeDtypeStruct((n_rows_pad, dw), jnp.uint32),
        input_output_aliases={2: 0},
        compiler_params=_cparams(("arbitrary",)),
        name="moe_dispatch",
    )(pos, tp, jnp.zeros((n_rows_pad, dw), jnp.uint32))


def _experts_kernel(te_ref, nu_ref, xs_ref, wg_ref, wu_ref, wd_ref, ys_ref, wg_s, wu_s, wd_s):
    i = pl.program_id(0)
    used = i < nu_ref[0]
    new_expert = jnp.logical_or(i == 0, te_ref[i] != te_ref[jnp.maximum(i - 1, 0)])

    @pl.when(jnp.logical_and(used, new_expert))
    def _():
        wg_s[...] = wg_ref[0, 0].astype(BF16)
        wu_s[...] = wu_ref[0, 0].astype(BF16)
        wd_s[...] = wd_ref[0, 0].astype(BF16)

    @pl.when(jnp.logical_not(used))
    def _():
        ys_ref[...] = jnp.zeros_like(ys_ref)

    @pl.when(used)
    def _():
        a, b = _unpack_bf16_pair(xs_ref[...])
        x = jnp.concatenate([a, b], axis=1)
        hg = jnp.dot(x, wg_s[...], preferred_element_type=F32)
        hu = jnp.dot(x, wu_s[...], preferred_element_type=F32)
        hid = (hg * jax.nn.sigmoid(hg) * hu).astype(BF16)
        ys_ref[...] = jnp.dot(hid, wd_s[...], preferred_element_type=F32)


def _experts(tile_expert, n_used, xs, wg, wu, wd, layer):
    rows, dw = xs.shape
    _, _, d, de = wg.shape
    grid_spec = pltpu.PrefetchScalarGridSpec(
        num_scalar_prefetch=2, grid=(rows // MOE_TILE,),
        in_specs=[pl.BlockSpec((MOE_TILE, dw), lambda i, te, nu: (i, 0)),
                  pl.BlockSpec((1, 1, d, de), lambda i, te, nu: (layer, te[i], 0, 0)),
                  pl.BlockSpec((1, 1, d, de), lambda i, te, nu: (layer, te[i], 0, 0)),
                  pl.BlockSpec((1, 1, de, d), lambda i, te, nu: (layer, te[i], 0, 0))],
        out_specs=pl.BlockSpec((MOE_TILE, d), lambda i, te, nu: (i, 0)),
        scratch_shapes=[pltpu.VMEM((d, de), BF16), pltpu.VMEM((d, de), BF16), pltpu.VMEM((de, d), BF16)])
    return pl.pallas_call(
        _experts_kernel,
        grid_spec=grid_spec,
        out_shape=jax.ShapeDtypeStruct((rows, d), F32),
        compiler_params=_cparams(("arbitrary",)),
        name="moe_experts",
    )(tile_expert, n_used, xs, wg, wu, wd)


COMBINE_ROWS = 32


def _combine_kernel(pos_ref, ys_ref, x1_ref, route_ref, *rest, tc, prompt_tiles):
    if prompt_tiles is None:
        o_ref, buf_ref, sem = rest
    else:
        g_ref, yp_ref, ysm_ref, buf_ref, sem = rest
    i = pl.program_id(0)
    n_steps = pl.num_programs(0)
    slot = i % 2

    def issue_rows(step, to_slot, row0, n_rows):
        for r in range(n_rows):
            for s in range(TOP_K):
                p = pos_ref[TOP_K * (step * tc + row0 + r) + s]
                pltpu.make_async_copy(ys_ref.at[pl.ds(p, 1)], buf_ref.at[to_slot, s, pl.ds(row0 + r, 1)],
                                      sem.at[to_slot]).start()

    @pl.when(i == 0)
    def _():
        lax.fori_loop(0, tc // COMBINE_ROWS,
                      lambda c, carry: (issue_rows(0, 0, c * COMBINE_ROWS, COMBINE_ROWS), carry)[1], 0)

    for s in range(TOP_K):
        pltpu.make_async_copy(ys_ref.at[pl.ds(0, tc)], buf_ref.at[slot, s], sem.at[slot]).wait()

    def run(prefetch, out_ref):
        def body(c, carry):
            r0 = pl.multiple_of(c * COMBINE_ROWS, COMBINE_ROWS)
            if prefetch:
                issue_rows(i + 1, 1 - slot, r0, COMBINE_ROWS)
            rows = pl.ds(r0, COMBINE_ROWS)
            route = route_ref[rows, :]
            x = (x1_ref[rows, :] + buf_ref[slot, 0, rows, :] * route[:, TOP_K:TOP_K + 1]
                 + buf_ref[slot, 1, rows, :] * route[:, TOP_K + 1:TOP_K + 2])
            if prompt_tiles is not None:
                ms = jnp.mean(x * x, axis=-1, keepdims=True)
                x = x * lax.rsqrt(ms + RMS_EPS) * g_ref[...]
            out_ref[rows, :] = x
            return carry
        lax.fori_loop(0, tc // COMBINE_ROWS, body, 0)

    more = i + 1 < n_steps
    for prefetch, cond in ((True, more), (False, jnp.logical_not(more))):
        if prompt_tiles is None:
            pl.when(cond)(functools.partial(run, prefetch, o_ref))
        else:
            pl.when(jnp.logical_and(cond, i < prompt_tiles))(functools.partial(run, prefetch, yp_ref))
            pl.when(jnp.logical_and(cond, i >= prompt_tiles))(functools.partial(run, prefetch, ysm_ref))


def _combine(pos, ys, x1, route, n_p, final_gain=None):
    n, d = x1.shape
    tc, prompt_tiles = _split_tiles(n_p, n - n_p, 256)
    assert tc % COMBINE_ROWS == 0
    row = lambda i, pos: (i, 0)
    in_specs = [pl.BlockSpec(memory_space=pl.ANY), pl.BlockSpec((tc, d), row), pl.BlockSpec((tc, LANES), row)]
    args = [pos, ys, x1, route]
    if final_gain is None:
        out_specs = pl.BlockSpec((tc, d), row)
        out_shape = jax.ShapeDtypeStruct((n, d), F32)
    else:
        in_specs.append(pl.BlockSpec((1, d), lambda i, pos: (0, 0)))
        args.append(final_gain.reshape(1, d))
        out_specs = [pl.BlockSpec((tc, d), lambda i, pos: (jnp.minimum(i, prompt_tiles - 1), 0)),
                     pl.BlockSpec((tc, d), lambda i, pos: (jnp.maximum(i - prompt_tiles, 0), 0))]
        out_shape = [jax.ShapeDtypeStruct((n_p, d), F32), jax.ShapeDtypeStruct((n - n_p, d), F32)]
    grid_spec = pltpu.PrefetchScalarGridSpec(
        num_scalar_prefetch=1, grid=(n // tc,), in_specs=in_specs, out_specs=out_specs,
        scratch_shapes=[pltpu.VMEM((2, TOP_K, tc, d), F32), pltpu.SemaphoreType.DMA((2,))])
    return pl.pallas_call(
        functools.partial(_combine_kernel, tc=tc, prompt_tiles=None if final_gain is None else prompt_tiles),
        grid_spec=grid_spec,
        out_shape=out_shape,
        compiler_params=_cparams(("arbitrary",)),
        name="moe_combine",
    )(*args)


def _moe(tp, route, x1, wg, wu, wd, layer, n_p, final_gain=None):
    n = x1.shape[0]
    n_tiles = -(-n * TOP_K // MOE_TILE) + N_EXPERTS
    pos, tile_expert, n_used = _moe_plan(route, n_tiles)
    xs = _dispatch(pos, tp, n_tiles * MOE_TILE)
    ys = _experts(tile_expert, n_used, xs, wg, wu, wd, layer)
    return _combine(pos, ys, x1, route, n_p, final_gain)


def _rope_tables(pos):
    half = HEAD_DIM_A // 2
    inv = ROPE_THETA ** (-jnp.arange(half, dtype=F32) / half)
    ang = pos.astype(F32)[:, None] * inv[None, :]
    cos = jnp.cos(ang)
    sin = jnp.sin(ang)
    reps = LANES // HEAD_DIM_A
    return jnp.tile(jnp.concatenate([cos, cos], axis=1), (1, reps)), \
        jnp.tile(jnp.concatenate([-sin, sin], axis=1), (1, reps))


def _router_weights(w_group, b_group, w_router, b_router):
    d = w_group.shape[0]
    w = jnp.concatenate([w_group, w_router], axis=1).astype(F32)
    n_log = w.shape[1]
    hi = w.astype(BF16)
    lo = (w - hi.astype(F32)).astype(BF16)
    wr = jnp.zeros((d, LANES), BF16)
    wr = wr.at[:, :n_log].set(hi).at[:, ROUTER_LO_LANE:ROUTER_LO_LANE + n_log].set(lo)
    br = jnp.zeros((1, LANES), F32).at[0, :n_log].set(jnp.concatenate([b_group, b_router]).astype(F32))
    return wr, br


def kernel(x_prompt, x_sample, cache_win_k, cache_win_v, cache_sb_k, cache_sb_v, norm_mix, norm_ffn, norm_final,
           a_w_qkv, a_b_qkv, a_sinks, a_w_o, b_w_qkv, b_w_o, moe_w_group, moe_b_group, moe_w_router,
           moe_b_router, moe_w_gate, moe_w_up, moe_w_down):
    bp, sp, d = x_prompt.shape
    bs, ts, _ = x_sample.shape
    n_p = bp * sp
    n_s = bs * ts
    past = cache_sb_k.shape[2]
    n_kv = cache_win_k.shape[3]
    n_heads_b = cache_sb_k.shape[3]
    nq_a = n_kv * GROUP_A * HEAD_DIM_A
    nk_a = n_kv * HEAD_DIM_A
    assert sp % CHUNK == 0 and past % CHUNK == 0 and ts <= CHUNK and n_p % ts == 0
    assert cache_win_k.shape[2] == WINDOW and d == n_heads_b * HEAD_DIM_B == nq_a

    x = (x_prompt.reshape(n_p, d), x_sample.reshape(n_s, d))
    pos = jnp.concatenate([jnp.tile(jnp.arange(sp, dtype=jnp.int32), bp),
                           jnp.tile(past + jnp.arange(ts, dtype=jnp.int32), bs)])
    cos, sin = _rope_tables(pos)
    u = (lax.broadcasted_iota(jnp.int32, (SB_BLOCK, SB_BLOCK), 0)
         > lax.broadcasted_iota(jnp.int32, (SB_BLOCK, SB_BLOCK), 1)).astype(BF16)

    outs = {}
    for i in range(2):
        if i == 0:
            tn = _pick(a_w_qkv.shape[2], 1280)
            qkv = _norm_proj(x, norm_mix[i], a_w_qkv[0].astype(BF16), a_b_qkv[0], cos, sin, nq_a + nk_a, tn)
            sinks = a_sinks[0].astype(F32)
            o_p = _win_prompt(qkv, sinks, bp, sp, n_kv)
            o_s = _win_sample(qkv, cache_win_k[0].reshape(bs, WINDOW, nk_a), cache_win_v[0].reshape(bs, WINDOW, nk_a),
                              sinks, n_p, bs, ts, n_kv)
            k_all = qkv[:, nq_a:nq_a + nk_a]
            v_all = qkv[:, nq_a + nk_a:]
            outs["wkp"] = k_all[:n_p].reshape(bp, sp, n_kv, HEAD_DIM_A)[:, sp - WINDOW:][None]
            outs["wvp"] = v_all[:n_p].reshape(bp, sp, n_kv, HEAD_DIM_A)[:, sp - WINDOW:][None]
            outs["wks"] = jnp.concatenate(
                [cache_win_k[0], k_all[n_p:].reshape(bs, ts, n_kv, HEAD_DIM_A)], axis=1)[:, -WINDOW:][None]
            outs["wvs"] = jnp.concatenate(
                [cache_win_v[0], v_all[n_p:].reshape(bs, ts, n_kv, HEAD_DIM_A)], axis=1)[:, -WINDOW:][None]
            w_o = a_w_o[0]
        else:
            zeros_b = jnp.zeros((b_w_qkv.shape[2],), F32)
            qkv = _norm_proj(x, norm_mix[i], b_w_qkv[0].astype(BF16), zeros_b, cos, sin, 0, d)
            o_p = _sb_prompt(qkv, u, bp, sp, n_heads_b)
            o_s = _sb_sample(qkv, cache_sb_k, cache_sb_v, u, n_p, bs, ts, n_heads_b)
            k_all = qkv[:, d:2 * d]
            v_all = qkv[:, 2 * d:]
            outs["skp"] = k_all[:n_p].reshape(1, bp, sp, n_heads_b, HEAD_DIM_B)
            outs["svp"] = v_all[:n_p].reshape(1, bp, sp, n_heads_b, HEAD_DIM_B)
            outs["sks"] = k_all[n_p:].reshape(1, bs, ts, n_heads_b, HEAD_DIM_B)
            outs["svs"] = v_all[n_p:].reshape(1, bs, ts, n_heads_b, HEAD_DIM_B)
            w_o = b_w_o[0]
        wr, br = _router_weights(moe_w_group[i], moe_b_group[i], moe_w_router[i], moe_b_router[i])
        x1, tp, route = _oproj_router(o_p, o_s, x, w_o.astype(BF16), norm_ffn[i], wr, br)
        x = _moe(tp, route, x1, moe_w_gate, moe_w_up, moe_w_down, i, n_p,
                 final_gain=norm_final if i == 1 else None)
    y_p, y_s = x
    return (y_p.reshape(bp, sp, d), y_s.reshape(bs, ts, d),
            outs["wkp"], outs["wvp"], outs["wks"], outs["wvs"],
            outs["skp"], outs["svp"], outs["sks"], outs["svs"])
```

```python
import functools
import math

import jax
import jax.numpy as jnp
from jax import lax
from jax.experimental import pallas as pl
from jax.experimental.pallas import tpu as pltpu

F32 = jnp.float32
BF16 = jnp.bfloat16

CHUNK = 64
WINDOW = 128
HEAD_DIM_A = 64
GROUP_A = 8
HEAD_DIM_B = 128
N_GROUPS = 4
EXPERTS_PER_GROUP = 8
N_EXPERTS = N_GROUPS * EXPERTS_PER_GROUP
ROPE_THETA = 10000.0
RMS_EPS = 1e-6
NEG_INF = -1e30

LANES = 128
ROUTER_LO_LANE = 64
GATE_LANE0 = N_GROUPS
VMEM_LIMIT = 56 * 1024 * 1024

NT_DIMS = (((1,), (1,)), ((), ()))


def _cparams(sem):
    return pltpu.CompilerParams(dimension_semantics=sem, vmem_limit_bytes=VMEM_LIMIT)


def _pick(n, pref, align=8):
    for t in range(min(n, pref), 0, -1):
        if n % t == 0 and t % align == 0:
            return t
    raise ValueError((n, pref, align))


def _norm_proj_kernel(*refs, rope_cols, tn, prompt_tiles):
    if prompt_tiles is None:
        x_ref, g_ref, w_ref, b_ref, cos_ref, sin_ref, o_ref, h_ref = refs
    else:
        x_ref, xs_ref, g_ref, w_ref, b_ref, cos_ref, sin_ref, o_ref, h_ref = refs
    j = pl.program_id(1)

    @pl.when(j == 0)
    def _():
        x = x_ref[...]
        if prompt_tiles is not None:
            x = jnp.where(pl.program_id(0) < prompt_tiles, x, xs_ref[...])
        ms = jnp.mean(x * x, axis=-1, keepdims=True)
        h_ref[...] = (x * lax.rsqrt(ms + RMS_EPS) * g_ref[...]).astype(BF16)

    y = jnp.dot(h_ref[...], w_ref[...], preferred_element_type=F32) + b_ref[...]
    if rope_cols:
        reps = tn // LANES
        cos = jnp.concatenate([cos_ref[...]] * reps, axis=1)
        sin = jnp.concatenate([sin_ref[...]] * reps, axis=1)
        lane = lax.broadcasted_iota(jnp.int32, y.shape, 1)
        half = HEAD_DIM_A // 2
        first = (lane % HEAD_DIM_A) < half
        swapped = jnp.where(first, pltpu.roll(y, tn - half, 1), pltpu.roll(y, half, 1))
        roped = y * cos + swapped * sin
        y = jnp.where(lane + j * tn < rope_cols, roped, y)
    o_ref[...] = y


def _split_tiles(n_p, n_s, pref):
    tm = _pick(math.gcd(n_p, n_s), pref, 16)
    return tm, n_p // tm


def _row_specs(x, pref):
    if not isinstance(x, tuple):
        n, d = x.shape
        tm = _pick(n, pref)
        return [x], [pl.BlockSpec((tm, d), lambda i, *_: (i, 0))], tm, n, None
    x_p, x_s = x
    d = x_p.shape[1]
    tm, prompt_tiles = _split_tiles(x_p.shape[0], x_s.shape[0], pref)
    specs = [pl.BlockSpec((tm, d), lambda i, *_: (jnp.minimum(i, prompt_tiles - 1), 0)),
             pl.BlockSpec((tm, d), lambda i, *_: (jnp.maximum(i - prompt_tiles, 0), 0))]
    return [x_p, x_s], specs, tm, x_p.shape[0] + x_s.shape[0], prompt_tiles


def _norm_proj(x, g, w, b, cos, sin, rope_cols, tn):
    x_args, x_specs, tm, n, prompt_tiles = _row_specs(x, 512)
    d, nout = w.shape
    kern = functools.partial(_norm_proj_kernel, rope_cols=rope_cols, tn=tn, prompt_tiles=prompt_tiles)
    return pl.pallas_call(
        kern,
        grid=(n // tm, nout // tn),
        in_specs=x_specs + [
            pl.BlockSpec((1, d), lambda i, j: (0, 0)),
            pl.BlockSpec((d, tn), lambda i, j: (0, j)),
            pl.BlockSpec((1, tn), lambda i, j: (0, j)),
            pl.BlockSpec((tm, LANES), lambda i, j: (i, 0)),
            pl.BlockSpec((tm, LANES), lambda i, j: (i, 0)),
        ],
        out_specs=pl.BlockSpec((tm, tn), lambda i, j: (i, j)),
        out_shape=jax.ShapeDtypeStruct((n, nout), F32),
        scratch_shapes=[pltpu.VMEM((tm, d), BF16)],
        compiler_params=_cparams(("arbitrary", "arbitrary")),
        name="norm_proj",
    )(*x_args, g.reshape(1, d), w, b.reshape(1, nout), cos, sin)


WIN_KEYS_PAD = 256
HALF = LANES // 2


def _sink_attention(q, k, v, sinks_ref, lo, n_keys, n_kv):
    t = q.shape[0]
    sp = WIN_KEYS_PAD
    assert HEAD_DIM_A == HALF and n_kv % 2 == 0 and n_keys < sp
    qb = (q * (HEAD_DIM_A ** -0.5)).astype(BF16)
    low_half = lax.broadcasted_iota(jnp.int32, (sp, LANES), 1) < HALF
    halves = (low_half, jnp.logical_not(low_half))
    ones = tuple(jnp.where(h, 1.0, 0.0).astype(BF16) for h in halves)
    col = lax.broadcasted_iota(jnp.int32, (1, sp), 1)
    visible = jnp.logical_and(col >= lo, col < n_keys)
    is_sink = col == n_keys

    kmat, vmat = {}, {}
    for slab in range(n_kv // 2):
        ks = k[:, slab * LANES:(slab + 1) * LANES]
        vs = v[:, slab * LANES:(slab + 1) * LANES]
        moved = (pltpu.roll(ks, HALF, 1), pltpu.roll(vs, HALF, 1))
        for gp in range(2):
            for par in range(2):
                src_k, src_v = (ks, vs) if par == gp else moved
                kmat[2 * slab + gp, par] = jnp.where(halves[par], src_k, 0.0).astype(BF16)
                vmat[2 * slab + gp, par] = jnp.concatenate(
                    [jnp.where(halves[par], src_v, 0.0).astype(BF16), ones[par]], axis=1)

    ppg = GROUP_A // 2
    outs = []
    for g in range(n_kv):
        qg = jnp.concatenate([qb[:, (g * ppg + i) * LANES:(g * ppg + i + 1) * LANES] for i in range(ppg)], axis=0)
        acc = None
        for par in range(2):
            s = lax.dot_general(qg, kmat[g, par], NT_DIMS, preferred_element_type=F32)
            s = jnp.where(visible, s, NEG_INF)
            s = jnp.concatenate(
                [jnp.where(is_sink, sinks_ref[2 * (g * ppg + i) + par], s[i * t:(i + 1) * t]) for i in range(ppg)],
                axis=0)
            p = jnp.exp(s - jnp.max(s, axis=-1, keepdims=True)).astype(BF16)
            o = jnp.dot(p, vmat[g, par], preferred_element_type=F32)
            acc = o if acc is None else acc + o
        o_norm = acc[:, :LANES] / acc[:, LANES:]
        outs.extend(o_norm[i * t:(i + 1) * t] for i in range(ppg))
    return jnp.concatenate(outs, axis=1)


def _win_prompt_kernel(sinks_ref, q_ref, k0_ref, k1_ref, k2_ref, v0_ref, v1_ref, v2_ref, o_ref, *, n_kv):
    c = pl.program_id(1)
    pad = jnp.zeros((WIN_KEYS_PAD - 3 * CHUNK, k0_ref.shape[1]), F32)
    k = jnp.concatenate([k0_ref[...], k1_ref[...], k2_ref[...], pad], axis=0)
    v = jnp.concatenate([v0_ref[...], v1_ref[...], v2_ref[...], pad], axis=0)
    lo = jnp.maximum(2 - c, 0) * CHUNK
    o_ref[...] = _sink_attention(q_ref[...], k, v, sinks_ref, lo, 3 * CHUNK, n_kv).astype(o_ref.dtype)


def _win_prompt(qkv, sinks, batch, seq, n_kv):
    nq = n_kv * GROUP_A * HEAD_DIM_A
    nk = n_kv * HEAD_DIM_A
    nc = seq // CHUNK
    kcol = nq // nk
    q_spec = pl.BlockSpec((CHUNK, nq), lambda b, c: (b * nc + c, 0))

    def kv_spec(back, col):
        return pl.BlockSpec((CHUNK, nk), lambda b, c: (b * nc + jnp.maximum(c - back, 0), col))

    return pl.pallas_call(
        functools.partial(_win_prompt_kernel, n_kv=n_kv),
        grid=(batch, nc),
        in_specs=[pl.BlockSpec(memory_space=pltpu.SMEM),
                  q_spec, kv_spec(2, kcol), kv_spec(1, kcol), kv_spec(0, kcol),
                  kv_spec(2, kcol + 1), kv_spec(1, kcol + 1), kv_spec(0, kcol + 1)],
        out_specs=pl.BlockSpec((CHUNK, nq), lambda b, c: (b * nc + c, 0)),
        out_shape=jax.ShapeDtypeStruct((batch * seq, nq), BF16),
        compiler_params=_cparams(("arbitrary", "arbitrary")),
        name="win_prompt",
    )(sinks, qkv, qkv, qkv, qkv, qkv, qkv, qkv)


def _win_sample_kernel(sinks_ref, q_ref, kn_ref, vn_ref, ck_ref, cv_ref, o_ref, *, n_kv):
    t = kn_ref.shape[0]
    pad = jnp.zeros((WIN_KEYS_PAD - WINDOW - t, kn_ref.shape[1]), F32)
    k = jnp.concatenate([ck_ref[0], kn_ref[...], pad], axis=0)
    v = jnp.concatenate([cv_ref[0], vn_ref[...], pad], axis=0)
    o_ref[...] = _sink_attention(q_ref[...], k, v, sinks_ref, 0, WINDOW + t, n_kv).astype(o_ref.dtype)


def _win_sample(qkv, cache_k, cache_v, sinks, row0, batch, t, n_kv):
    nq = n_kv * GROUP_A * HEAD_DIM_A
    nk = n_kv * HEAD_DIM_A
    kcol = nq // nk
    blk0 = row0 // t
    return pl.pallas_call(
        functools.partial(_win_sample_kernel, n_kv=n_kv),
        grid=(batch,),
        in_specs=[pl.BlockSpec(memory_space=pltpu.SMEM),
                  pl.BlockSpec((t, nq), lambda b: (blk0 + b, 0)),
                  pl.BlockSpec((t, nk), lambda b: (blk0 + b, kcol)),
                  pl.BlockSpec((t, nk), lambda b: (blk0 + b, kcol + 1)),
                  pl.BlockSpec((1, WINDOW, nk), lambda b: (b, 0, 0)),
                  pl.BlockSpec((1, WINDOW, nk), lambda b: (b, 0, 0))],
        out_specs=pl.BlockSpec((t, nq), lambda b: (b, 0)),
        out_shape=jax.ShapeDtypeStruct((batch * t, nq), BF16),
        compiler_params=_cparams(("arbitrary",)),
        name="win_sample",
    )(sinks, qkv, qkv, qkv, cache_k, cache_v)


SB_STOP_BELOW = -110.0
SB_BLOCK = 256


def _sb_weights(z, mask, carry, u):
    sp = jnp.maximum(z, 0.0) + jnp.log(1.0 + jnp.exp(-jnp.abs(z)))
    ls = -sp if mask is None else jnp.where(mask, -sp, 0.0)
    hi = ls.astype(BF16)
    lo = (ls - hi.astype(F32)).astype(BF16)
    after = (jnp.dot(hi, u, preferred_element_type=F32) + jnp.dot(lo, u, preferred_element_type=F32)) + carry
    a = jnp.exp(z - sp + after)
    if mask is not None:
        a = jnp.where(mask, a, 0.0)
    return a, carry + jnp.sum(ls, axis=-1, keepdims=True)


def _sb_prompt_kernel(q_ref, k_ref, v_ref, u_ref, o_ref, *, seq):
    tb = SB_BLOCK
    scale = HEAD_DIM_B ** -0.5
    u = u_ref[...]
    row = lax.broadcasted_iota(jnp.int32, (tb, tb), 0)
    col = lax.broadcasted_iota(jnp.int32, (tb, tb), 1)
    strictly_earlier = col < row
    no_carry = jnp.zeros((tb, 1), F32)

    def queries(start):
        return (q_ref[pl.ds(start, tb), :] * scale).astype(BF16)

    def scores(q, start, size):
        kb = k_ref[pl.ds(start, size), :].astype(BF16)
        return lax.dot_general(q, kb, NT_DIMS, preferred_element_type=F32)

    def weighted_values(a, start, size):
        vb = v_ref[pl.ds(start, size), :].astype(BF16)
        return jnp.dot(a.astype(BF16), vb, preferred_element_type=F32)

    a, _ = _sb_weights(scores(queries(0), 0, tb), strictly_earlier, no_carry, u)
    o_ref[pl.ds(0, tb), :] = weighted_values(a, 0, tb).astype(o_ref.dtype)

    def q_block(qi, _):
        q0 = pl.multiple_of(qi * tb, tb)
        p0 = pl.multiple_of(q0 - tb, tb)
        q = queries(q0)
        z = scores(q, p0, 2 * tb)
        a_own, carry = _sb_weights(z[:, tb:], strictly_earlier, no_carry, u)
        a_prev, carry = _sb_weights(z[:, :tb], None, carry, u)
        acc = weighted_values(jnp.concatenate([a_prev, a_own], axis=1), p0, 2 * tb)

        def more(state):
            kj, cmax, _, _ = state
            return jnp.logical_and(kj >= 0, cmax > SB_STOP_BELOW)

        def earlier_block(state):
            kj, _, carry, acc = state
            k0 = pl.multiple_of(kj * tb, tb)
            a, carry = _sb_weights(scores(q, k0, tb), None, carry, u)
            return kj - 1, jnp.max(carry), carry, acc + weighted_values(a, k0, tb)

        _, _, _, acc = lax.while_loop(more, earlier_block, (qi - 2, jnp.max(carry), carry, acc))
        o_ref[pl.ds(q0, tb), :] = acc.astype(o_ref.dtype)
        return 0

    lax.fori_loop(1, seq // tb, q_block, 0)


def _sb_prompt(qkv, u, batch, seq, n_heads):
    hd = HEAD_DIM_B
    assert seq % SB_BLOCK == 0
    return pl.pallas_call(
        functools.partial(_sb_prompt_kernel, seq=seq),
        grid=(batch, n_heads),
        in_specs=[pl.BlockSpec((seq, hd), lambda b, h: (b, h)),
                  pl.BlockSpec((seq, hd), lambda b, h: (b, n_heads + h)),
                  pl.BlockSpec((seq, hd), lambda b, h: (b, 2 * n_heads + h)),
                  pl.BlockSpec((SB_BLOCK, SB_BLOCK), lambda b, h: (0, 0))],
        out_specs=pl.BlockSpec((seq, hd), lambda b, h: (b, h)),
        out_shape=jax.ShapeDtypeStruct((batch * seq, n_heads * hd), BF16),
        compiler_params=_cparams(("arbitrary", "arbitrary")),
        name="sb_prompt",
    )(qkv, qkv, qkv, u)


SB_NEW_PAD = SB_BLOCK


def _sb_sample_kernel(q_ref, kn_ref, vn_ref, u_ref, ck_ref, cv_ref, o_ref, kbuf, vbuf, sem,
                      *, t, n_heads, n_blocks):
    b = pl.program_id(0)
    tb = SB_BLOCK
    hd = HEAD_DIM_B
    d = n_heads * hd
    rows = n_heads * t
    scale = hd ** -0.5
    u = u_ref[...]

    def block_copies(j, slot, batch=b):
        start = pl.multiple_of((n_blocks - 1 - j) * tb, tb)
        out = []
        for h in range(n_heads):
            out.append(pltpu.make_async_copy(ck_ref.at[0, batch, pl.ds(start, tb), h, :], kbuf.at[slot, h],
                                             sem.at[slot, 0]))
            out.append(pltpu.make_async_copy(cv_ref.at[0, batch, pl.ds(start, tb), h, :], vbuf.at[slot, h],
                                             sem.at[slot, 1]))
        return out

    @pl.when(b == 0)
    def _():
        for c in block_copies(0, 0):
            c.start()

    q = q_ref[...] * scale
    qh = [q[:, h * hd:(h + 1) * hd].astype(BF16) for h in range(n_heads)]

    pad = jnp.zeros((SB_NEW_PAD - t, d), F32)
    kn = jnp.concatenate([kn_ref[...], pad], axis=0).astype(BF16)
    vn = jnp.concatenate([vn_ref[...], pad], axis=0).astype(BF16)
    z = jnp.concatenate(
        [lax.dot_general(qh[h], kn[:, h * hd:(h + 1) * hd], NT_DIMS, preferred_element_type=F32)
         for h in range(n_heads)], axis=0)
    rr = lax.broadcasted_iota(jnp.int32, (rows, SB_NEW_PAD), 0) % t
    cc = lax.broadcasted_iota(jnp.int32, (rows, SB_NEW_PAD), 1)
    a, carry = _sb_weights(z, cc < rr, jnp.zeros((rows, 1), F32), u[:SB_NEW_PAD, :SB_NEW_PAD])
    a = a.astype(BF16)
    acc = jnp.concatenate(
        [jnp.dot(a[h * t:(h + 1) * t], vn[:, h * hd:(h + 1) * hd], preferred_element_type=F32)
         for h in range(n_heads)], axis=1)

    def more(state):
        j, cmax, _, _ = state
        return jnp.logical_and(j < n_blocks, cmax > SB_STOP_BELOW)

    def cache_block(state):
        j, _, carry, acc = state
        slot = j % 2
        for c in block_copies(j, slot):
            c.wait()

        @pl.when(j + 1 < n_blocks)
        def _():
            for c in block_copies(j + 1, 1 - slot):
                c.start()

        z = jnp.concatenate(
            [lax.dot_general(qh[h], kbuf[slot, h].astype(BF16), NT_DIMS, preferred_element_type=F32)
             for h in range(n_heads)], axis=0)
        a, carry = _sb_weights(z, None, carry, u)
        a = a.astype(BF16)
        out = jnp.concatenate(
            [jnp.dot(a[h * t:(h + 1) * t], vbuf[slot, h].astype(BF16), preferred_element_type=F32)
             for h in range(n_heads)], axis=1)
        return j + 1, jnp.max(carry), carry, acc + out

    j, _, _, acc = lax.while_loop(more, cache_block, (jnp.int32(0), jnp.max(carry), carry, acc))
    o_ref[...] = acc.astype(o_ref.dtype)

    @pl.when(j < n_blocks)
    def _():
        for c in block_copies(j, j % 2):
            c.wait()

    @pl.when(b + 1 < pl.num_programs(0))
    def _():
        for c in block_copies(0, 0, b + 1):
            c.start()


def _sb_sample(qkv, cache_k, cache_v, u, row0, batch, t, n_heads):
    hd = HEAD_DIM_B
    d = n_heads * hd
    past = cache_k.shape[2]
    assert past % SB_BLOCK == 0 and t <= SB_NEW_PAD
    blk0 = row0 // t
    return pl.pallas_call(
        functools.partial(_sb_sample_kernel, t=t, n_heads=n_heads, n_blocks=past // SB_BLOCK),
        grid=(batch,),
        in_specs=[pl.BlockSpec((t, d), lambda b: (blk0 + b, 0)),
                  pl.BlockSpec((t, d), lambda b: (blk0 + b, 1)),
                  pl.BlockSpec((t, d), lambda b: (blk0 + b, 2)),
                  pl.BlockSpec((SB_BLOCK, SB_BLOCK), lambda b: (0, 0)),
                  pl.BlockSpec(memory_space=pl.ANY),
                  pl.BlockSpec(memory_space=pl.ANY)],
        out_specs=pl.BlockSpec((t, d), lambda b: (b, 0)),
        out_shape=jax.ShapeDtypeStruct((batch * t, d), BF16),
        scratch_shapes=[pltpu.VMEM((2, n_heads, SB_BLOCK, hd), F32),
                        pltpu.VMEM((2, n_heads, SB_BLOCK, hd), F32),
                        pltpu.SemaphoreType.DMA((2, 2))],
        compiler_params=_cparams(("arbitrary",)),
        name="sb_sample",
    )(qkv, qkv, qkv, u, cache_k, cache_v)


def _pack_bf16_pair(a, b):
    ab = lax.bitcast_convert_type(a.astype(BF16).astype(F32), jnp.uint32)
    bb = lax.bitcast_convert_type(b.astype(BF16).astype(F32), jnp.uint32)
    return (ab >> 16) | (bb & jnp.uint32(0xFFFF0000))


def _unpack_bf16_pair(p):
    a = lax.bitcast_convert_type(p << 16, F32)
    b = lax.bitcast_convert_type(p & jnp.uint32(0xFFFF0000), F32)
    return a.astype(BF16), b.astype(BF16)


def _oproj_router_kernel(op_ref, os_ref, *refs, prompt_tiles, split_x):
    if split_x:
        x_ref, xs_ref, wo_ref, g_ref, wr_ref, br_ref, x1_ref, tp_ref, route_ref = refs
    else:
        x_ref, wo_ref, g_ref, wr_ref, br_ref, x1_ref, tp_ref, route_ref = refs
    in_prompt = pl.program_id(0) < prompt_tiles
    o = jnp.where(in_prompt, op_ref[...], os_ref[...])
    x = x_ref[...]
    if split_x:
        x = jnp.where(in_prompt, x, xs_ref[...])
    x1 = x + jnp.dot(o, wo_ref[...], preferred_element_type=F32)
    x1_ref[...] = x1
    ms = jnp.mean(x1 * x1, axis=-1, keepdims=True)
    t = x1 * lax.rsqrt(ms + RMS_EPS) * g_ref[...]
    half = t.shape[1] // 2
    tp_ref[...] = _pack_bf16_pair(t[:, :half], t[:, half:])
    hi = t.astype(BF16)
    lo = (t - hi.astype(F32)).astype(BF16)
    r_hi = jnp.dot(hi, wr_ref[...], preferred_element_type=F32)
    r_lo = jnp.dot(lo, wr_ref[...], preferred_element_type=F32)
    logits = r_hi + pltpu.roll(r_hi, LANES - ROUTER_LO_LANE, 1) + r_lo + br_ref[...]

    lane = lax.broadcasted_iota(jnp.int32, logits.shape, 1)
    lane_f = lane.astype(F32)
    big = float(LANES)
    is_group = lane < N_GROUPS
    gl = jnp.where(is_group, logits, -jnp.inf)
    gmax = jnp.max(gl, axis=-1, keepdims=True)
    gidx = jnp.min(jnp.where(gl == gmax, lane_f, big), axis=-1, keepdims=True)
    gsum = jnp.sum(jnp.where(is_group, jnp.exp(logits - gmax), 0.0), axis=-1, keepdims=True)
    g_w = 1.0 / gsum
    expert = lane - GATE_LANE0
    in_group = (expert >= 0) & (expert < N_EXPERTS) & ((expert // EXPERTS_PER_GROUP).astype(F32) == gidx)
    el = jnp.where(in_group, logits, -jnp.inf)
    v1 = jnp.max(el, axis=-1, keepdims=True)
    i1 = jnp.min(jnp.where(el == v1, lane_f, big), axis=-1, keepdims=True)
    el2 = jnp.where(lane_f == i1, -jnp.inf, el)
    v2 = jnp.max(el2, axis=-1, keepdims=True)
    i2 = jnp.min(jnp.where(el2 == v2, lane_f, big), axis=-1, keepdims=True)
    e21 = jnp.exp(v2 - v1)
    w1 = g_w / (1.0 + e21)
    w2 = g_w * e21 / (1.0 + e21)
    route_ref[...] = jnp.where(lane == 0, i1 - GATE_LANE0,
                               jnp.where(lane == 1, i2 - GATE_LANE0,
                                         jnp.where(lane == 2, w1, jnp.where(lane == 3, w2, 0.0))))


def _oproj_router(o_p, o_s, x, wo, g, wr, br):
    d = wo.shape[0]
    o_args, o_specs, tm, n, prompt_tiles = _row_specs((o_p, o_s), 256)
    if isinstance(x, tuple):
        x_args, x_specs = list(x), list(o_specs)
    else:
        x_args, x_specs = [x], [pl.BlockSpec((tm, d), lambda i: (i, 0))]
    row = lambda i: (i, 0)
    const = lambda i: (0, 0)
    return pl.pallas_call(
        functools.partial(_oproj_router_kernel, prompt_tiles=prompt_tiles, split_x=isinstance(x, tuple)),
        grid=(n // tm,),
        in_specs=o_specs + x_specs + [
                  pl.BlockSpec((d, d), const), pl.BlockSpec((1, d), const),
                  pl.BlockSpec((d, LANES), const), pl.BlockSpec((1, LANES), const)],
        out_specs=[pl.BlockSpec((tm, d), row), pl.BlockSpec((tm, d // 2), row), pl.BlockSpec((tm, LANES), row)],
        out_shape=[jax.ShapeDtypeStruct((n, d), F32), jax.ShapeDtypeStruct((n, d // 2), jnp.uint32),
                   jax.ShapeDtypeStruct((n, LANES), F32)],
        compiler_params=_cparams(("arbitrary",)),
        name="oproj_router",
    )(*o_args, *x_args, wo, g.reshape(1, d), wr, br)


MOE_TILE = 256
TOP_K = 2


def _moe_plan(route, n_tiles):
    n = route.shape[0]
    e_flat = route[:, :TOP_K].astype(jnp.int32).reshape(n * TOP_K)
    onehot = (e_flat[:, None] == jnp.arange(N_EXPERTS, dtype=jnp.int32)[None, :]).astype(jnp.int32)
    csum = jnp.cumsum(onehot, axis=0)
    rank = jnp.sum(csum * onehot, axis=1) - 1
    counts = csum[-1]
    tiles = (counts + MOE_TILE - 1) // MOE_TILE
    tile_end = jnp.cumsum(tiles)
    row_start = (tile_end - tiles) * MOE_TILE
    pos = jnp.sum(onehot * row_start[None, :], axis=1) + rank
    n_used = tile_end[-1]
    tile_id = jnp.minimum(jnp.arange(n_tiles, dtype=jnp.int32), n_used - 1)
    tile_expert = jnp.sum((tile_end[None, :] <= tile_id[:, None]).astype(jnp.int32), axis=1)
    return pos.astype(jnp.int32), tile_expert.astype(jnp.int32), n_used.reshape(1).astype(jnp.int32)


def _dispatch_kernel(pos_ref, tp_ref, xs_in_ref, xs_ref, sem, *, tc):
    del xs_in_ref
    base = pl.program_id(0) * tc

    def issue(j, carry):
        for s in range(TOP_K):
            p = pos_ref[TOP_K * (base + j) + s]
            pltpu.make_async_copy(tp_ref.at[pl.ds(j, 1)], xs_ref.at[pl.ds(p, 1)], sem).start()
        return carry

    lax.fori_loop(0, tc, issue, 0)
    for _ in range(TOP_K):
        pltpu.make_async_copy(tp_ref, xs_ref.at[pl.ds(0, tc)], sem).wait()


def _dispatch(pos, tp, n_rows_pad):
    n, dw = tp.shape
    tc = _pick(n, 1536)
    assert tc <= n_rows_pad
    grid_spec = pltpu.PrefetchScalarGridSpec(
        num_scalar_prefetch=1, grid=(n // tc,),
        in_specs=[pl.BlockSpec((tc, dw), lambda i, pos: (i, 0)), pl.BlockSpec(memory_space=pl.ANY)],
        out_specs=pl.BlockSpec(memory_space=pl.ANY),
        scratch_shapes=[pltpu.SemaphoreType.DMA(())])
    return pl.pallas_call(
        functools.partial(_dispatch_kernel, tc=tc),
        grid_spec=grid_spec,
        out_shape=jax.ShapeDtypeStruct((n_rows_pad, dw), jnp.uint32),
        input_output_aliases={2: 0},
        compiler_params=_cparams(("arbitrary",)),
        name="moe_dispatch",
    )(pos, tp, jnp.zeros((n_rows_pad, dw), jnp.uint32))


def _experts_kernel(te_ref, nu_ref, xs_ref, wg_ref, wu_ref, wd_ref, ys_ref, wg_s, wu_s, wd_s):
    i = pl.program_id(0)
    used = i < nu_ref[0]
    new_expert = jnp.logical_or(i == 0, te_ref[i] != te_ref[jnp.maximum(i - 1, 0)])

    @pl.when(jnp.logical_and(used, new_expert))
    def _():
        wg_s[...] = wg_ref[0, 0].astype(BF16)
        wu_s[...] = wu_ref[0, 0].astype(BF16)
        wd_s[...] = wd_ref[0, 0].astype(BF16)

    @pl.when(jnp.logical_not(used))
    def _():
        ys_ref[...] = jnp.zeros_like(ys_ref)

    @pl.when(used)
    def _():
        a, b = _unpack_bf16_pair(xs_ref[...])
        x = jnp.concatenate([a, b], axis=1)
        hg = jnp.dot(x, wg_s[...], preferred_element_type=F32)
        hu = jnp.dot(x, wu_s[...], preferred_element_type=F32)
        hid = (hg * jax.nn.sigmoid(hg) * hu).astype(BF16)
        ys_ref[...] = jnp.dot(hid, wd_s[...], preferred_element_type=F32)


def _experts(tile_expert, n_used, xs, wg, wu, wd, layer):
    rows, dw = xs.shape
    _, _, d, de = wg.shape
    grid_spec = pltpu.PrefetchScalarGridSpec(
        num_scalar_prefetch=2, grid=(rows // MOE_TILE,),
        in_specs=[pl.BlockSpec((MOE_TILE, dw), lambda i, te, nu: (i, 0)),
                  pl.BlockSpec((1, 1, d, de), lambda i, te, nu: (layer, te[i], 0, 0)),
                  pl.BlockSpec((1, 1, d, de), lambda i, te, nu: (layer, te[i], 0, 0)),
                  pl.BlockSpec((1, 1, de, d), lambda i, te, nu: (layer, te[i], 0, 0))],
        out_specs=pl.BlockSpec((MOE_TILE, d), lambda i, te, nu: (i, 0)),
        scratch_shapes=[pltpu.VMEM((d, de), BF16), pltpu.VMEM((d, de), BF16), pltpu.VMEM((de, d), BF16)])
    return pl.pallas_call(
        _experts_kernel,
        grid_spec=grid_spec,
        out_shape=jax.ShapeDtypeStruct((rows, d), F32),
        compiler_params=_cparams(("arbitrary",)),
        name="moe_experts",
    )(tile_expert, n_used, xs, wg, wu, wd)


COMBINE_ROWS = 64


def _combine_kernel(pos_ref, ys_ref, x1_ref, route_ref, *rest, tc, prompt_tiles):
    if prompt_tiles is None:
        o_ref, buf_ref, sem = rest
    else:
        g_ref, yp_ref, ysm_ref, buf_ref, sem = rest
    i = pl.program_id(0)
    n_steps = pl.num_programs(0)
    slot = i % 2

    def issue_rows(step, to_slot, row0, n_rows):
        for r in range(n_rows):
            for s in range(TOP_K):
                p = pos_ref[TOP_K * (step * tc + row0 + r) + s]
                pltpu.make_async_copy(ys_ref.at[pl.ds(p, 1)], buf_ref.at[to_slot, s, pl.ds(row0 + r, 1)],
                                      sem.at[to_slot]).start()

    @pl.when(i == 0)
    def _():
        lax.fori_loop(0, tc // COMBINE_ROWS,
                      lambda c, carry: (issue_rows(0, 0, c * COMBINE_ROWS, COMBINE_ROWS), carry)[1], 0)

    for s in range(TOP_K):
        pltpu.make_async_copy(ys_ref.at[pl.ds(0, tc)], buf_ref.at[slot, s], sem.at[slot]).wait()

    def run(prefetch, out_ref):
        def body(c, carry):
            r0 = pl.multiple_of(c * COMBINE_ROWS, COMBINE_ROWS)
            if prefetch:
                issue_rows(i + 1, 1 - slot, r0, COMBINE_ROWS)
            rows = pl.ds(r0, COMBINE_ROWS)
            route = route_ref[rows, :]
            x = (x1_ref[rows, :] + buf_ref[slot, 0, rows, :] * route[:, TOP_K:TOP_K + 1]
                 + buf_ref[slot, 1, rows, :] * route[:, TOP_K + 1:TOP_K + 2])
            if prompt_tiles is not None:
                ms = jnp.mean(x * x, axis=-1, keepdims=True)
                x = x * lax.rsqrt(ms + RMS_EPS) * g_ref[...]
            out_ref[rows, :] = x
            return carry
        lax.fori_loop(0, tc // COMBINE_ROWS, body, 0)

    more = i + 1 < n_steps
    for prefetch, cond in ((True, more), (False, jnp.logical_not(more))):
        if prompt_tiles is None:
            pl.when(cond)(functools.partial(run, prefetch, o_ref))
        else:
            pl.when(jnp.logical_and(cond, i < prompt_tiles))(functools.partial(run, prefetch, yp_ref))
            pl.when(jnp.logical_and(cond, i >= prompt_tiles))(functools.partial(run, prefetch, ysm_ref))


def _combine(pos, ys, x1, route, n_p, final_gain=None):
    n, d = x1.shape
    tc, prompt_tiles = _split_tiles(n_p, n - n_p, 256)
    assert tc % COMBINE_ROWS == 0
    row = lambda i, pos: (i, 0)
    in_specs = [pl.BlockSpec(memory_space=pl.ANY), pl.BlockSpec((tc, d), row), pl.BlockSpec((tc, LANES), row)]
    args = [pos, ys, x1, route]
    if final_gain is None:
        out_specs = pl.BlockSpec((tc, d), row)
        out_shape = jax.ShapeDtypeStruct((n, d), F32)
    else:
        in_specs.append(pl.BlockSpec((1, d), lambda i, pos: (0, 0)))
        args.append(final_gain.reshape(1, d))
        out_specs = [pl.BlockSpec((tc, d), lambda i, pos: (jnp.minimum(i, prompt_tiles - 1), 0)),
                     pl.BlockSpec((tc, d), lambda i, pos: (jnp.maximum(i - prompt_tiles, 0), 0))]
        out_shape = [jax.ShapeDtypeStruct((n_p, d), F32), jax.ShapeDtypeStruct((n - n_p, d), F32)]
    grid_spec = pltpu.PrefetchScalarGridSpec(
        num_scalar_prefetch=1, grid=(n // tc,), in_specs=in_specs, out_specs=out_specs,
        scratch_shapes=[pltpu.VMEM((2, TOP_K, tc, d), F32), pltpu.SemaphoreType.DMA((2,))])
    return pl.pallas_call(
        functools.partial(_combine_kernel, tc=tc, prompt_tiles=None if final_gain is None else prompt_tiles),
        grid_spec=grid_spec,
        out_shape=out_shape,
        compiler_params=_cparams(("arbitrary",)),
        name="moe_combine",
    )(*args)


def _moe(tp, route, x1, wg, wu, wd, layer, n_p, final_gain=None):
    n = x1.shape[0]
    n_tiles = -(-n * TOP_K // MOE_TILE) + N_EXPERTS
    pos, tile_expert, n_used = _moe_plan(route, n_tiles)
    xs = _dispatch(pos, tp, n_tiles * MOE_TILE)
    ys = _experts(tile_expert, n_used, xs, wg, wu, wd, layer)
    return _combine(pos, ys, x1, route, n_p, final_gain)


def _rope_tables(pos):
    half = HEAD_DIM_A // 2
    inv = ROPE_THETA ** (-jnp.arange(half, dtype=F32) / half)
    ang = pos.astype(F32)[:, None] * inv[None, :]
    cos = jnp.cos(ang)
    sin = jnp.sin(ang)
    reps = LANES // HEAD_DIM_A
    return jnp.tile(jnp.concatenate([cos, cos], axis=1), (1, reps)), \
        jnp.tile(jnp.concatenate([-sin, sin], axis=1), (1, reps))


def _router_weights(w_group, b_group, w_router, b_router):
    d = w_group.shape[0]
    w = jnp.concatenate([w_group, w_router], axis=1).astype(F32)
    n_log = w.shape[1]
    hi = w.astype(BF16)
    lo = (w - hi.astype(F32)).astype(BF16)
    wr = jnp.zeros((d, LANES), BF16)
    wr = wr.at[:, :n_log].set(hi).at[:, ROUTER_LO_LANE:ROUTER_LO_LANE + n_log].set(lo)
    br = jnp.zeros((1, LANES), F32).at[0, :n_log].set(jnp.concatenate([b_group, b_router]).astype(F32))
    return wr, br


def kernel(x_prompt, x_sample, cache_win_k, cache_win_v, cache_sb_k, cache_sb_v, norm_mix, norm_ffn, norm_final,
           a_w_qkv, a_b_qkv, a_sinks, a_w_o, b_w_qkv, b_w_o, moe_w_group, moe_b_group, moe_w_router,
           moe_b_router, moe_w_gate, moe_w_up, moe_w_down):
    bp, sp, d = x_prompt.shape
    bs, ts, _ = x_sample.shape
    n_p = bp * sp
    n_s = bs * ts
    past = cache_sb_k.shape[2]
    n_kv = cache_win_k.shape[3]
    n_heads_b = cache_sb_k.shape[3]
    nq_a = n_kv * GROUP_A * HEAD_DIM_A
    nk_a = n_kv * HEAD_DIM_A
    assert sp % CHUNK == 0 and past % CHUNK == 0 and ts <= CHUNK and n_p % ts == 0
    assert cache_win_k.shape[2] == WINDOW and d == n_heads_b * HEAD_DIM_B == nq_a

    x = (x_prompt.reshape(n_p, d), x_sample.reshape(n_s, d))
    pos = jnp.concatenate([jnp.tile(jnp.arange(sp, dtype=jnp.int32), bp),
                           jnp.tile(past + jnp.arange(ts, dtype=jnp.int32), bs)])
    cos, sin = _rope_tables(pos)
    u = (lax.broadcasted_iota(jnp.int32, (SB_BLOCK, SB_BLOCK), 0)
         > lax.broadcasted_iota(jnp.int32, (SB_BLOCK, SB_BLOCK), 1)).astype(BF16)

    outs = {}
    for i in range(2):
        if i == 0:
            tn = _pick(a_w_qkv.shape[2], 1280)
            qkv = _norm_proj(x, norm_mix[i], a_w_qkv[0].astype(BF16), a_b_qkv[0], cos, sin, nq_a + nk_a, tn)
            sinks = a_sinks[0].astype(F32)
            o_p = _win_prompt(qkv, sinks, bp, sp, n_kv)
            o_s = _win_sample(qkv, cache_win_k[0].reshape(bs, WINDOW, nk_a), cache_win_v[0].reshape(bs, WINDOW, nk_a),
                              sinks, n_p, bs, ts, n_kv)
            k_all = qkv[:, nq_a:nq_a + nk_a]
            v_all = qkv[:, nq_a + nk_a:]
            outs["wkp"] = k_all[:n_p].reshape(bp, sp, n_kv, HEAD_DIM_A)[:, sp - WINDOW:][None]
            outs["wvp"] = v_all[:n_p].reshape(bp, sp, n_kv, HEAD_DIM_A)[:, sp - WINDOW:][None]
            outs["wks"] = jnp.concatenate(
                [cache_win_k[0], k_all[n_p:].reshape(bs, ts, n_kv, HEAD_DIM_A)], axis=1)[:, -WINDOW:][None]
            outs["wvs"] = jnp.concatenate(
                [cache_win_v[0], v_all[n_p:].reshape(bs, ts, n_kv, HEAD_DIM_A)], axis=1)[:, -WINDOW:][None]
            w_o = a_w_o[0]
        else:
            zeros_b = jnp.zeros((b_w_qkv.shape[2],), F32)
            qkv = _norm_proj(x, norm_mix[i], b_w_qkv[0].astype(BF16), zeros_b, cos, sin, 0, d)
            o_p = _sb_prompt(qkv, u, bp, sp, n_heads_b)
            o_s = _sb_sample(qkv, cache_sb_k, cache_sb_v, u, n_p, bs, ts, n_heads_b)
            k_all = qkv[:, d:2 * d]
            v_all = qkv[:, 2 * d:]
            outs["skp"] = k_all[:n_p].reshape(1, bp, sp, n_heads_b, HEAD_DIM_B)
            outs["svp"] = v_all[:n_p].reshape(1, bp, sp, n_heads_b, HEAD_DIM_B)
            outs["sks"] = k_all[n_p:].reshape(1, bs, ts, n_heads_b, HEAD_DIM_B)
            outs["svs"] = v_all[n_p:].reshape(1, bs, ts, n_heads_b, HEAD_DIM_B)
            w_o = b_w_o[0]
        wr, br = _router_weights(moe_w_group[i], moe_b_group[i], moe_w_router[i], moe_b_router[i])
        x1, tp, route = _oproj_router(o_p, o_s, x, w_o.astype(BF16), norm_ffn[i], wr, br)
        x = _moe(tp, route, x1, moe_w_gate, moe_w_up, moe_w_down, i, n_p,
                 final_gain=norm_final if i == 1 else None)
    y_p, y_s = x
    return (y_p.reshape(bp, sp, d), y_s.reshape(bs, ts, d),
            outs["wkp"], outs["wvp"], outs["wks"], outs["wvs"],
            outs["skp"], outs["svp"], outs["sks"], outs["svs"])
```
